```python
import math
import jax, jax.numpy as jnp
from jax import lax
import numpy as np

D_MODEL = 2048
BATCH = 4
SEQ = 4096
DEPTH = 2

MEM_LEN = 256
NORM_EPS = 1e-6

DA_HEADS = 8
DA_QK_DIM = 64
DA_V_DIM = 2 * DA_QK_DIM
DA_WIDTH = DA_HEADS * DA_V_DIM
Q_BLOCK = 128

HG_HEADS = 8
HG_DK = 128
HG_DV = 128
HG_WIDTH = HG_HEADS * HG_DV
HG_CHUNK = 64

MIX_WIDTH = DA_WIDTH + HG_WIDTH

DA_Q_COLS = DA_HEADS * 2 * DA_QK_DIM
DA_K_COLS = DA_HEADS * 2 * DA_QK_DIM
DA_V_COLS = DA_WIDTH
HG_KEY_COLS = HG_HEADS * HG_DK
HG_VAL_COLS = HG_HEADS * HG_DV
IN_SIZES = (DA_Q_COLS, DA_K_COLS, DA_V_COLS, HG_KEY_COLS, HG_KEY_COLS, HG_VAL_COLS, HG_VAL_COLS)
IN_COLS = sum(IN_SIZES)
IN_SPLITS = tuple(int(v) for v in np.cumsum(IN_SIZES)[:-1])

REL_BUCKETS = 32
REL_MAX_DIST = 128

CX_HEADS = 4
CX_HEAD_DIM = 128
CX_WIDTH = CX_HEADS * CX_HEAD_DIM

FFN_HIDDEN = ((8 * D_MODEL // 3 + 255) // 256) * 256

kernel_name = "hymba_style_diffattn_hgrn2_hybrid"


def rms_norm(x, g):
    xf = x.astype(jnp.float32)
    y = xf * lax.rsqrt(jnp.mean(xf * xf, axis=-1, keepdims=True) + NORM_EPS)
    return (y * g.astype(jnp.float32)).astype(x.dtype)


def t5_causal_bucket(dist):
    n = jnp.maximum(dist, 0)
    max_exact = REL_BUCKETS // 2
    nf = jnp.maximum(n, 1).astype(jnp.float32)
    large = max_exact + (jnp.log(nf / max_exact) / math.log(REL_MAX_DIST / max_exact)
                         * (REL_BUCKETS - max_exact)).astype(jnp.int32)
    large = jnp.minimum(large, REL_BUCKETS - 1)
    return jnp.where(n < max_exact, n, large)


def differential_attention(q, k, v, lam, lam_init, subln_g, rel_bias):
    B, S = q.shape[0], q.shape[1]
    n_blk = S // Q_BLOCK
    scale = DA_QK_DIM ** -0.5
    kt = jnp.transpose(k, (0, 2, 3, 1, 4))
    vt = jnp.transpose(v, (0, 2, 1, 3))
    qb = q.reshape(B, n_blk, Q_BLOCK, DA_HEADS, 2, DA_QK_DIM).transpose(1, 0, 3, 4, 2, 5)
    k_pos = jnp.arange(S)

    def block(args):
        q_blk, blk = args
        q_pos = blk * Q_BLOCK + jnp.arange(Q_BLOCK)
        dist = q_pos[:, None] - k_pos[None, :]
        bias = jnp.transpose(rel_bias[t5_causal_bucket(dist)], (2, 0, 1)).astype(jnp.float32)
        s = jnp.einsum('bhmqd,bhmkd->bhmqk', q_blk, kt).astype(jnp.float32) * scale + bias[None, :, None]
        s = jnp.where(dist >= 0, s, -jnp.inf)
        p = jax.nn.softmax(s, axis=-1)
        a = p[:, :, 0] - lam * p[:, :, 1]
        return jnp.einsum('bhqk,bhkd->bhqd', a.astype(vt.dtype), vt)

    o = lax.map(block, (qb, jnp.arange(n_blk)))
    o = rms_norm(o, subln_g) * (1.0 - lam_init)
    return o.transpose(1, 0, 3, 2, 4).reshape(B, S, DA_WIDTH)


def hgrn2(f_logit, q, i, g, lb, onorm_g):
    B, S = q.shape[0], q.shape[1]
    n_chunk = S // HG_CHUNK
    f = lb.astype(jnp.float32) + (1.0 - lb.astype(jnp.float32)) * jax.nn.sigmoid(f_logit.astype(jnp.float32))
    log_f = jnp.log(f)
    k = 1.0 - f

    def to_chunks(t, d):
        return t.astype(jnp.float32).reshape(B, n_chunk, HG_CHUNK, HG_HEADS, d).transpose(1, 0, 3, 2, 4)

    qc, kc, vc, gc = to_chunks(q, HG_DK), to_chunks(k, HG_DK), to_chunks(i, HG_DV), to_chunks(log_f, HG_DK)
    causal = jnp.tril(jnp.ones((HG_CHUNK, HG_CHUNK), dtype=bool))

    def step(state, inp):
        qt, kt, vt, gt = inp
        b = jnp.cumsum(gt, axis=2)
        diff = b[:, :, :, None, :] - b[:, :, None, :, :]
        decay = jnp.exp(jnp.where(causal[:, :, None], diff, -jnp.inf))
        attn = jnp.einsum('bhtc,bhsc,bhtsc->bhts', qt, kt, decay)
        o = (jnp.einsum('bhts,bhsv->bhtv', attn, vt)
             + jnp.einsum('bhtc,bhcv->bhtv', qt * jnp.exp(b), state))
        b_last = b[:, :, -1:, :]
        state = (jnp.exp(b_last[:, :, 0, :])[..., None] * state
                 + jnp.einsum('bhsc,bhsv->bhcv', kt * jnp.exp(b_last - b), vt))
        return state, o

    state0 = jnp.zeros((B, HG_HEADS, HG_DK, HG_DV), jnp.float32)
    _, outs = lax.scan(step, state0, (qc, kc, vc, gc))
    o = outs.transpose(1, 0, 3, 2, 4).reshape(B, S, HG_HEADS, HG_DV).astype(q.dtype)
    o = rms_norm(o, onorm_g) * jax.nn.silu(g.reshape(B, S, HG_HEADS, HG_DV))
    return o.reshape(B, S, HG_WIDTH)


def setup_inputs(seed: int = 0) -> dict:
    key = jax.random.key(seed)
    ks = jax.random.split(key, 24)

    def w(k, shape, fan_in):
        return jax.random.normal(k, shape, jnp.float32) * fan_in ** -0.5

    def gain(k, shape):
        return 1.0 + 0.02 * jax.random.normal(k, shape, jnp.float32)

    return {
        "x": jax.random.normal(ks[0], (BATCH, SEQ, D_MODEL), jnp.float32),
        "mem": jax.random.normal(ks[1], (BATCH, MEM_LEN, D_MODEL), jnp.float32),
        "w_in": w(ks[2], (DEPTH, D_MODEL, IN_COLS), D_MODEL),
        "w_out": w(ks[3], (DEPTH, MIX_WIDTH, D_MODEL), MIX_WIDTH),
        "w_cq": w(ks[4], (DEPTH, D_MODEL, CX_WIDTH), D_MODEL),
        "w_ckv": w(ks[5], (DEPTH, D_MODEL, 2 * CX_WIDTH), D_MODEL),
        "w_co": w(ks[6], (DEPTH, CX_WIDTH, D_MODEL), CX_WIDTH),
        "w_ffn_in": w(ks[7], (DEPTH, D_MODEL, 2 * FFN_HIDDEN), D_MODEL),
        "w_ffn_out": w(ks[8], (DEPTH, FFN_HIDDEN, D_MODEL), FFN_HIDDEN),
        "mix_pre_g": gain(ks[9], (DEPTH, D_MODEL)),
        "mix_post_g": gain(ks[10], (DEPTH, D_MODEL)),
        "cross_pre_g": gain(ks[11], (DEPTH, D_MODEL)),
        "cross_post_g": gain(ks[12], (DEPTH, D_MODEL)),
        "mem_norm_g": gain(ks[13], (DEPTH, D_MODEL)),
        "ffn_pre_g": gain(ks[14], (DEPTH, D_MODEL)),
        "ffn_post_g": gain(ks[15], (DEPTH, D_MODEL)),
        "da_subln_g": gain(ks[16], (DEPTH, DA_V_DIM)),
        "hg_onorm_g": gain(ks[17], (DEPTH, HG_DV)),
        "lambda_q1": 0.1 * jax.random.normal(ks[18], (DEPTH, DA_QK_DIM), jnp.float32),
        "lambda_k1": 0.1 * jax.random.normal(ks[19], (DEPTH, DA_QK_DIM), jnp.float32),
        "lambda_q2": 0.1 * jax.random.normal(ks[20], (DEPTH, DA_QK_DIM), jnp.float32),
        "lambda_k2": 0.1 * jax.random.normal(ks[21], (DEPTH, DA_QK_DIM), jnp.float32),
        "hg_lb_logits": 0.5 * jax.random.normal(ks[22], (DEPTH, HG_KEY_COLS), jnp.float32),
        "rel_bias": 0.1 * jax.random.normal(ks[23], (REL_BUCKETS, DA_HEADS), jnp.float32),
    }


def reference(x, mem, w_in, w_out, w_cq, w_ckv, w_co, w_ffn_in, w_ffn_out,
              mix_pre_g, mix_post_g, cross_pre_g, cross_post_g, mem_norm_g,
              ffn_pre_g, ffn_post_g, da_subln_g, hg_onorm_g,
              lambda_q1, lambda_k1, lambda_q2, lambda_k2, hg_lb_logits, rel_bias):
    B, S = x.shape[0], x.shape[1]
    M = mem.shape[1]
    lb_p = jax.nn.softmax(hg_lb_logits.astype(jnp.float32), axis=0)
    lower_bounds = jnp.cumsum(lb_p, axis=0) - lb_p[0:1]

    for l in range(DEPTH):
        h = rms_norm(x, mix_pre_g[l])
        proj = h @ w_in[l]
        q_da, k_da, v_da, f_hg, q_hg, i_hg, g_hg = jnp.split(proj, IN_SPLITS, axis=-1)

        lam_init = 0.8 - 0.6 * math.exp(-0.3 * l)
        lam = (jnp.exp(jnp.sum(lambda_q1[l].astype(jnp.float32) * lambda_k1[l].astype(jnp.float32)))
               - jnp.exp(jnp.sum(lambda_q2[l].astype(jnp.float32) * lambda_k2[l].astype(jnp.float32)))
               + lam_init)
        o_da = differential_attention(
            q_da.reshape(B, S, DA_HEADS, 2, DA_QK_DIM),
            k_da.reshape(B, S, DA_HEADS, 2, DA_QK_DIM),
            v_da.reshape(B, S, DA_HEADS, DA_V_DIM),
            lam, lam_init, da_subln_g[l], rel_bias)
        o_hg = hgrn2(f_hg, q_hg, i_hg, g_hg, lower_bounds[l], hg_onorm_g[l])
        mixed = jnp.concatenate([o_da, o_hg], axis=-1) @ w_out[l]
        x = x + rms_norm(mixed, mix_post_g[l])

        h = rms_norm(x, cross_pre_g[l])
        m = rms_norm(mem, mem_norm_g[l])
        qx = (h @ w_cq[l]).reshape(B, S, CX_HEADS, CX_HEAD_DIM)
        kv = (m @ w_ckv[l]).reshape(B, M, 2, CX_HEADS, CX_HEAD_DIM)
        s = jnp.einsum('bqhd,bkhd->bhqk', qx, kv[:, :, 0]).astype(jnp.float32) * CX_HEAD_DIM ** -0.5
        p = jax.nn.softmax(s, axis=-1)
        o = jnp.einsum('bhqk,bkhd->bqhd', p.astype(x.dtype), kv[:, :, 1]).reshape(B, S, CX_WIDTH)
        x = x + rms_norm(o @ w_co[l], cross_post_g[l])

        h = rms_norm(x, ffn_pre_g[l])
        gate, up = jnp.split(h @ w_ffn_in[l], 2, axis=-1)
        x = x + rms_norm((jax.nn.silu(gate) * up) @ w_ffn_out[l], ffn_post_g[l])

    return x
```

```python
import functools
import math

import jax
import jax.numpy as jnp
from jax import lax
from jax.experimental import pallas as pl
from jax.experimental.pallas import tpu as pltpu

NORM_EPS = 1e-6

DA_HEADS = 8
DA_QK_DIM = 64
DA_V_DIM = 2 * DA_QK_DIM
DA_WIDTH = DA_HEADS * DA_V_DIM
HG_HEADS = 8
HG_DK = 128
HG_DV = 128
HG_WIDTH = HG_HEADS * HG_DV
REL_BUCKETS = 32
REL_MAX_DIST = 128
CX_HEADS = 4
CX_HEAD_DIM = 128
CX_WIDTH = CX_HEADS * CX_HEAD_DIM

LANES = 128
MASK_VALUE = -1e30
VMEM_LIMIT = 56 * 1024 * 1024

MM_TM = 1024
MM_TN = 512
OUT_TM = 512
CX_TM = 512
FFN_TM = 512
FFN_TH = 512
DA_TQ = 512
HG_TC = 256
HG_SB = 16

_NT = (((1,), (1,)), ((), ()))
_TN = (((0,), (0,)), ((), ()))

f32 = jnp.float32
bf16 = jnp.bfloat16


def _rms(x, g):
    return x * lax.rsqrt(jnp.mean(x * x, axis=-1, keepdims=True) + NORM_EPS) * g


def _sigmoid(x):
    return 1.0 / (1.0 + jnp.exp(-x))


def _params(sem):
    return pltpu.CompilerParams(dimension_semantics=sem, vmem_limit_bytes=VMEM_LIMIT)


def _norm_matmul_kernel(x_ref, g_ref, w_ref, o_ref, h_ref):
    @pl.when(pl.program_id(1) == 0)
    def _():
        h_ref[...] = _rms(x_ref[...], g_ref[...]).astype(bf16)

    o_ref[...] = jnp.dot(h_ref[...], w_ref[...], preferred_element_type=f32).astype(o_ref.dtype)


def _norm_matmul(x, g, w, out_dtype):
    t, d = x.shape
    n = w.shape[1]
    tm, tn = min(MM_TM, t), min(MM_TN, n)
    assert t % tm == 0 and n % tn == 0
    return pl.pallas_call(
        _norm_matmul_kernel,
        grid=(t // tm, n // tn),
        in_specs=[
            pl.BlockSpec((tm, d), lambda i, j: (i, 0)),
            pl.BlockSpec((1, d), lambda i, j: (0, 0)),
            pl.BlockSpec((d, tn), lambda i, j: (0, j)),
        ],
        out_specs=pl.BlockSpec((tm, tn), lambda i, j: (i, j)),
        out_shape=jax.ShapeDtypeStruct((t, n), out_dtype),
        scratch_shapes=[pltpu.VMEM((tm, d), bf16)],
        compiler_params=_params(("parallel", "arbitrary")),
        name="norm_matmul",
    )(x, g.reshape(1, d), w)


def _mix_out_kernel(a1_ref, a2_ref, w1_ref, w2_ref, g_ref, x_ref, o_ref):
    y = jnp.dot(a1_ref[...], w1_ref[...], preferred_element_type=f32)
    y = y + jnp.dot(a2_ref[...], w2_ref[...], preferred_element_type=f32)
    o_ref[...] = x_ref[...] + _rms(y, g_ref[...])


def _mix_out(a1, a2, w, g, x):
    t, d = x.shape
    k1, k2 = a1.shape[1], a2.shape[1]
    assert k1 == k2 and w.shape[0] == k1 + k2
    tm = min(OUT_TM, t)
    assert t % tm == 0
    return pl.pallas_call(
        _mix_out_kernel,
        grid=(t // tm,),
        in_specs=[
            pl.BlockSpec((tm, k1), lambda i: (i, 0)),
            pl.BlockSpec((tm, k2), lambda i: (i, 0)),
            pl.BlockSpec((k1, d), lambda i: (0, 0)),
            pl.BlockSpec((k2, d), lambda i: (1, 0)),
            pl.BlockSpec((1, d), lambda i: (0, 0)),
            pl.BlockSpec((tm, d), lambda i: (i, 0)),
        ],
        out_specs=pl.BlockSpec((tm, d), lambda i: (i, 0)),
        out_shape=jax.ShapeDtypeStruct((t, d), f32),
        compiler_params=_params(("parallel",)),
        name="mix_out",
    )(a1, a2, w, w, g.reshape(1, d), x)


def _t5_causal_bucket(dist):
    n = jnp.maximum(dist, 0)
    max_exact = REL_BUCKETS // 2
    nf = jnp.maximum(n, 1).astype(f32)
    large = max_exact + (jnp.log(nf / max_exact) / math.log(REL_MAX_DIST / max_exact)
                         * (REL_BUCKETS - max_exact)).astype(jnp.int32)
    large = jnp.minimum(large, REL_BUCKETS - 1)
    return jnp.where(n < max_exact, n, large)


def _da_bias_tiles(rel_bias, tq):
    d0 = jnp.arange(tq)[:, None] - jnp.arange(tq)[None, :]
    b0 = jnp.where((d0 >= 0)[:, :, None], rel_bias[_t5_causal_bucket(d0)].astype(f32), MASK_VALUE)
    b1 = rel_bias[_t5_causal_bucket(d0 + tq)].astype(f32)
    return jnp.transpose(jnp.stack([b0, b1]), (3, 0, 1, 2))


def _da_kernel(q_ref, k_ref, v_ref, bias_ref, far_ref, lq1_ref, lk1_ref, lq2_ref, lk2_ref, g_ref,
               o_ref, m_ref, l_ref, acc_ref, *, tq, lam_init):
    i = pl.program_id(2)
    q = q_ref[...]
    lane = lax.broadcasted_iota(jnp.int32, q.shape, 1)
    scale = jnp.asarray(DA_QK_DIM ** -0.5, q.dtype)
    zero = jnp.zeros_like(q)
    q_maps = (jnp.where(lane < DA_QK_DIM, q * scale, zero), jnp.where(lane >= DA_QK_DIM, q * scale, zero))

    m_ref[...] = jnp.full(m_ref.shape, MASK_VALUE, f32)
    l_ref[...] = jnp.zeros(l_ref.shape, f32)
    acc_ref[...] = jnp.zeros(acc_ref.shape, f32)

    def step(j, bias):
        rows = pl.ds(pl.multiple_of(j * tq, tq), tq)
        kb = k_ref[rows, :]
        vb = v_ref[rows, :]
        for mi in range(2):
            s = lax.dot_general(q_maps[mi], kb, _NT, preferred_element_type=f32) + bias
            m_prev = m_ref[mi]
            m_new = jnp.maximum(m_prev, jnp.max(s, axis=-1, keepdims=True))
            alpha = jnp.exp(m_prev - m_new)
            p = jnp.exp(s - m_new)
            l_ref[mi] = alpha * l_ref[mi] + jnp.sum(p, axis=-1, keepdims=True)
            acc_ref[mi] = alpha * acc_ref[mi] + jnp.dot(p.astype(vb.dtype), vb, preferred_element_type=f32)
            m_ref[mi] = m_new

    def far_body(j, carry):
        step(j, far_ref[...])
        return carry

    lax.fori_loop(0, jnp.maximum(i - 1, 0), far_body, 0)

    @pl.when(i >= 1)
    def _():
        step(i - 1, bias_ref[1])

    step(i, bias_ref[0])

    lam = (jnp.exp(jnp.sum(lq1_ref[...] * lk1_ref[...], axis=-1, keepdims=True))
           - jnp.exp(jnp.sum(lq2_ref[...] * lk2_ref[...], axis=-1, keepdims=True)) + lam_init)
    o = acc_ref[0] / l_ref[0] - lam * (acc_ref[1] / l_ref[1])
    o_ref[...] = (_rms(o, g_ref[...]) * (1.0 - lam_init)).astype(o_ref.dtype)


def _diff_attention(proj, rel_bias, lq1, lk1, lq2, lk2, subln_g, *, batch, seq, lam_init):
    tq = min(DA_TQ, seq)
    assert seq % tq == 0 and tq >= REL_MAX_DIST
    nq = seq // tq
    bias = _da_bias_tiles(rel_bias, tq)
    far = jnp.broadcast_to(rel_bias[REL_BUCKETS - 1].astype(f32)[:, None, None], (DA_HEADS, 1, tq))
    vec = lambda a: a.reshape(1, DA_QK_DIM).astype(f32)
    lam_spec = pl.BlockSpec((1, DA_QK_DIM), lambda b, h, i: (0, 0))
    return pl.pallas_call(
        functools.partial(_da_kernel, tq=tq, lam_init=lam_init),
        grid=(batch, DA_HEADS, nq),
        in_specs=[
            pl.BlockSpec((tq, DA_V_DIM), lambda b, h, i: (b * nq + i, h)),
            pl.BlockSpec((seq, DA_V_DIM), lambda b, h, i: (b, DA_HEADS + h)),
            pl.BlockSpec((seq, DA_V_DIM), lambda b, h, i: (b, 2 * DA_HEADS + h)),
            pl.BlockSpec((None, 2, tq, tq), lambda b, h, i: (h, 0, 0, 0)),
            pl.BlockSpec((None, 1, tq), lambda b, h, i: (h, 0, 0)),
            lam_spec, lam_spec, lam_spec, lam_spec,
            pl.BlockSpec((1, DA_V_DIM), lambda b, h, i: (0, 0)),
        ],
        out_specs=pl.BlockSpec((tq, DA_V_DIM), lambda b, h, i: (b * nq + i, h)),
        out_shape=jax.ShapeDtypeStruct((batch * seq, DA_WIDTH), bf16),
        scratch_shapes=[
            pltpu.VMEM((2, tq, 1), f32),
            pltpu.VMEM((2, tq, 1), f32),
            pltpu.VMEM((2, tq, DA_V_DIM), f32),
        ],
        compiler_params=_params(("parallel", "parallel", "arbitrary")),
        name="diff_attention",
    )(proj, proj, proj, bias, far, vec(lq1), vec(lk1), vec(lq2), vec(lk2), subln_g.reshape(1, DA_V_DIM))


def _hg_kernel(f_ref, q_ref, i_ref, g_ref, lbl_ref, on_ref, o_ref,
               st_ref, k_ref, b_ref, oacc_ref, *, layer, tc, sb):
    @pl.when(pl.program_id(2) == 0)
    def _():
        st_ref[...] = jnp.zeros(st_ref.shape, f32)

    lg = lbl_ref[...]
    pe = jnp.exp(lg - jnp.max(lg, axis=0, keepdims=True))
    lb = jnp.zeros((1, lg.shape[1]), f32)
    for r in range(1, layer + 1):
        lb = lb + pe[r:r + 1]
    lb = lb / jnp.sum(pe, axis=0, keepdims=True)

    f = lb + (1.0 - lb) * _sigmoid(f_ref[...])
    k_ref[...] = 1.0 - f
    b = jnp.log(f)
    r = lax.broadcasted_iota(jnp.int32, b.shape, 0) % sb
    sh = 1
    while sh < sb:
        b = b + jnp.where(r >= sh, pltpu.roll(b, sh, axis=0), 0.0)
        sh *= 2
    b_ref[...] = b

    trow = lax.broadcasted_iota(jnp.int32, (sb, 1), 0)

    def body(n, carry):
        rows = pl.ds(pl.multiple_of(n * sb, sb), sb)
        qb, kb, bb, vb = q_ref[rows, :], k_ref[rows, :], b_ref[rows, :], i_ref[rows, :]
        be = bb[sb - 1:sb, :]
        st = st_ref[...]
        qd = qb * jnp.exp(bb)
        kd = kb * jnp.exp(be - bb)
        o = lax.dot_general(qd.astype(bf16), st.astype(bf16), _NT, preferred_element_type=f32)
        for s in range(sb):
            e = jnp.exp(jnp.minimum(bb - bb[s:s + 1, :], 0.0))
            a = jnp.sum(qb * kb[s:s + 1, :] * e, axis=-1, keepdims=True)
            o = o + jnp.where(trow >= s, a, 0.0) * vb[s:s + 1, :]
        oacc_ref[rows, :] = o
        st_ref[...] = jnp.exp(be) * st + lax.dot_general(
            vb.astype(bf16), kd.astype(bf16), _TN, preferred_element_type=f32)
        return carry

    lax.fori_loop(0, tc // sb, body, 0)

    gate = g_ref[...]
    o_ref[...] = (_rms(oacc_ref[...], on_ref[...]) * (gate * _sigmoid(gate))).astype(o_ref.dtype)


def _hgrn2(proj, lb_logits, onorm_g, *, batch, seq, layer):
    tc = min(HG_TC, seq)
    assert seq % tc == 0 and tc % HG_SB == 0
    nc = seq // tc
    depth = lb_logits.shape[0]
    col = lambda part: pl.BlockSpec((tc, HG_DK), lambda b, h, c: (b * nc + c, part * HG_HEADS + h))
    return pl.pallas_call(
        functools.partial(_hg_kernel, layer=layer, tc=tc, sb=HG_SB),
        grid=(batch, HG_HEADS, nc),
        in_specs=[
            col(0), col(1), col(2), col(3),
            pl.BlockSpec((depth, HG_DK), lambda b, h, c: (0, h)),
            pl.BlockSpec((1, HG_DV), lambda b, h, c: (0, 0)),
        ],
        out_specs=pl.BlockSpec((tc, HG_DV), lambda b, h, c: (b * nc + c, h)),
        out_shape=jax.ShapeDtypeStruct((batch * seq, HG_WIDTH), bf16),
        scratch_shapes=[
            pltpu.VMEM((HG_DV, HG_DK), f32),
            pltpu.VMEM((tc, HG_DK), f32),
            pltpu.VMEM((tc, HG_DK), f32),
            pltpu.VMEM((tc, HG_DV), f32),
        ],
        compiler_params=_params(("parallel", "parallel", "arbitrary")),
        name="hgrn2",
    )(proj, proj, proj, proj, lb_logits.astype(f32), onorm_g.reshape(1, HG_DV))


def _cross_kernel(x_ref, gpre_ref, wq_ref, kv_ref, wo_ref, gpost_ref, o_ref):
    x = x_ref[...]
    h = _rms(x, gpre_ref[...]).astype(bf16)
    q = jnp.dot(h, wq_ref[...], preferred_element_type=f32).astype(bf16)
    outs = []
    for hd in range(CX_HEADS):
        cols = slice(hd * CX_HEAD_DIM, (hd + 1) * CX_HEAD_DIM)
        kh = kv_ref[:, cols]
        vh = kv_ref[:, CX_WIDTH + hd * CX_HEAD_DIM:CX_WIDTH + (hd + 1) * CX_HEAD_DIM]
        s = lax.dot_general(q[:, cols], kh, _NT, preferred_element_type=f32) * CX_HEAD_DIM ** -0.5
        p = jnp.exp(s - jnp.max(s, axis=-1, keepdims=True))
        oh = jnp.dot(p.astype(bf16), vh, preferred_element_type=f32)
        outs.append((oh / jnp.sum(p, axis=-1, keepdims=True)).astype(bf16))
    o = jnp.concatenate(outs, axis=-1)
    y = jnp.dot(o, wo_ref[...], preferred_element_type=f32)
    o_ref[...] = x + _rms(y, gpost_ref[...])


def _cross_attention(x, kv, wq, wo, gpre, gpost, *, batch, seq, mem_len):
    t, d = x.shape
    tm = min(CX_TM, seq)
    assert seq % tm == 0
    per_batch = seq // tm
    return pl.pallas_call(
        _cross_kernel,
        grid=(t // tm,),
        in_specs=[
            pl.BlockSpec((tm, d), lambda i: (i, 0)),
            pl.BlockSpec((1, d), lambda i: (0, 0)),
            pl.BlockSpec((d, CX_WIDTH), lambda i: (0, 0)),
            pl.BlockSpec((mem_len, 2 * CX_WIDTH), lambda i: (i // per_batch, 0)),
            pl.BlockSpec((CX_WIDTH, d), lambda i: (0, 0)),
            pl.BlockSpec((1, d), lambda i: (0, 0)),
        ],
        out_specs=pl.BlockSpec((tm, d), lambda i: (i, 0)),
        out_shape=jax.ShapeDtypeStruct((t, d), f32),
        compiler_params=_params(("parallel",)),
        name="cross_attention",
    )(x, gpre.reshape(1, d), wq, kv, wo, gpost.reshape(1, d))


def _ffn_kernel(x_ref, gpre_ref, wg_ref, wu_ref, wo_ref, gpost_ref, o_ref, h_ref, acc_ref):
    j = pl.program_id(1)

    @pl.when(j == 0)
    def _():
        h_ref[...] = _rms(x_ref[...], gpre_ref[...]).astype(bf16)
        acc_ref[...] = jnp.zeros(acc_ref.shape, f32)

    h = h_ref[...]
    gate = jnp.dot(h, wg_ref[...], preferred_element_type=f32)
    up = jnp.dot(h, wu_ref[...], preferred_element_type=f32)
    a = (gate * _sigmoid(gate) * up).astype(bf16)
    acc_ref[...] += jnp.dot(a, wo_ref[...], preferred_element_type=f32)

    @pl.when(j == pl.num_programs(1) - 1)
    def _():
        o_ref[...] = x_ref[...] + _rms(acc_ref[...], gpost_ref[...])


def _ffn(x, w_in, w_out, gpre, gpost):
    t, d = x.shape
    hidden = w_out.shape[0]
    tm, th = min(FFN_TM, t), min(FFN_TH, hidden)
    assert t % tm == 0 and hidden % th == 0 and w_in.shape[1] == 2 * hidden
    nh = hidden // th
    return pl.pallas_call(
        _ffn_kernel,
        grid=(t // tm, nh),
        in_specs=[
            pl.BlockSpec((tm, d), lambda i, j: (i, 0)),
            pl.BlockSpec((1, d), lambda i, j: (0, 0)),
            pl.BlockSpec((d, th), lambda i, j: (0, j)),
            pl.BlockSpec((d, th), lambda i, j: (0, nh + j)),
            pl.BlockSpec((th, d), lambda i, j: (j, 0)),
            pl.BlockSpec((1, d), lambda i, j: (0, 0)),
        ],
        out_specs=pl.BlockSpec((tm, d), lambda i, j: (i, 0)),
        out_shape=jax.ShapeDtypeStruct((t, d), f32),
        scratch_shapes=[pltpu.VMEM((tm, d), bf16), pltpu.VMEM((tm, d), f32)],
        compiler_params=_params(("parallel", "arbitrary")),
        name="ffn",
    )(x, gpre.reshape(1, d), w_in, w_in, w_out, gpost.reshape(1, d))


def kernel(x, mem, w_in, w_out, w_cq, w_ckv, w_co, w_ffn_in, w_ffn_out, mix_pre_g, mix_post_g, cross_pre_g, cross_post_g, mem_norm_g, ffn_pre_g, ffn_post_g, da_subln_g, hg_onorm_g, lambda_q1, lambda_k1, lambda_q2, lambda_k2, hg_lb_logits, rel_bias):
    batch, seq, d = x.shape
    mem_len = mem.shape[1]
    depth = w_in.shape[0]
    da_cols = 3 * DA_WIDTH
    xt = x.reshape(batch * seq, d)
    memt = mem.reshape(batch * mem_len, d)

    for l in range(depth):
        lam_init = 0.8 - 0.6 * math.exp(-0.3 * l)
        proj_da = _norm_matmul(xt, mix_pre_g[l], w_in[l, :, :da_cols].astype(bf16), bf16)
        proj_hg = _norm_matmul(xt, mix_pre_g[l], w_in[l, :, da_cols:].astype(bf16), f32)
        o_da = _diff_attention(proj_da, rel_bias, lambda_q1[l], lambda_k1[l], lambda_q2[l], lambda_k2[l],
                               da_subln_g[l], batch=batch, seq=seq, lam_init=lam_init)
        o_hg = _hgrn2(proj_hg, hg_lb_logits, hg_onorm_g[l], batch=batch, seq=seq, layer=l)
        xt = _mix_out(o_da, o_hg, w_out[l].astype(bf16), mix_post_g[l], xt)
        kv = _norm_matmul(memt, mem_norm_g[l], w_ckv[l].astype(bf16), bf16)
        xt = _cross_attention(xt, kv, w_cq[l].astype(bf16), w_co[l].astype(bf16),
                              cross_pre_g[l], cross_post_g[l], batch=batch, seq=seq, mem_len=mem_len)
        xt = _ffn(xt, w_ffn_in[l].astype(bf16), w_ffn_out[l].astype(bf16), ffn_pre_g[l], ffn_post_g[l])

    return xt.reshape(batch, seq, d)
```

```python
import functools
import math

import jax
import jax.numpy as jnp
from jax import lax
from jax.experimental import pallas as pl
from jax.experimental.pallas import tpu as pltpu

NORM_EPS = 1e-6

DA_HEADS = 8
DA_QK_DIM = 64
DA_V_DIM = 2 * DA_QK_DIM
DA_WIDTH = DA_HEADS * DA_V_DIM
HG_HEADS = 8
HG_DK = 128
HG_DV = 128
HG_WIDTH = HG_HEADS * HG_DV
REL_BUCKETS = 32
REL_MAX_DIST = 128
CX_HEADS = 4
CX_HEAD_DIM = 128
CX_WIDTH = CX_HEADS * CX_HEAD_DIM

LANES = 128
SUBLANES = 8
MASK_VALUE = -1e30
VMEM_LIMIT = 56 * 1024 * 1024

MM_TM = 1024
MM_TN = 512
OUT_TM = 512
CX_TM = 512
FFN_TM = 512
FFN_TH = 512
DA_TQ = 512
HG_TC = 256
HG_SB = 16
HG_UNROLL = 8

_NT = (((1,), (1,)), ((), ()))
_TN = (((0,), (0,)), ((), ()))

f32 = jnp.float32
bf16 = jnp.bfloat16


def _rms(x, g):
    return x * lax.rsqrt(jnp.mean(x * x, axis=-1, keepdims=True) + NORM_EPS) * g


def _sigmoid(x):
    return 1.0 / (1.0 + jnp.exp(-x))


def _params(sem):
    return pltpu.CompilerParams(dimension_semantics=sem, vmem_limit_bytes=VMEM_LIMIT)


def _norm_matmul_kernel(x_ref, g_ref, w_ref, o_ref, h_ref):
    @pl.when(pl.program_id(1) == 0)
    def _():
        h_ref[...] = _rms(x_ref[...], g_ref[...]).astype(bf16)

    o_ref[...] = jnp.dot(h_ref[...], w_ref[...], preferred_element_type=f32).astype(o_ref.dtype)


def _norm_matmul(x, g, w, out_dtype):
    t, d = x.shape
    n = w.shape[1]
    tm, tn = min(MM_TM, t), min(MM_TN, n)
    assert t % tm == 0 and n % tn == 0
    return pl.pallas_call(
        _norm_matmul_kernel,
        grid=(t // tm, n // tn),
        in_specs=[
            pl.BlockSpec((tm, d), lambda i, j: (i, 0)),
            pl.BlockSpec((1, d), lambda i, j: (0, 0)),
            pl.BlockSpec((d, tn), lambda i, j: (0, j)),
        ],
        out_specs=pl.BlockSpec((tm, tn), lambda i, j: (i, j)),
        out_shape=jax.ShapeDtypeStruct((t, n), out_dtype),
        scratch_shapes=[pltpu.VMEM((tm, d), bf16)],
        compiler_params=_params(("parallel", "arbitrary")),
        name="norm_matmul",
    )(x, g.reshape(1, d), w)


def _mix_out_kernel(a1_ref, a2_ref, w1_ref, w2_ref, g_ref, x_ref, o_ref):
    y = jnp.dot(a1_ref[...], w1_ref[...], preferred_element_type=f32)
    y = y + jnp.dot(a2_ref[...], w2_ref[...], preferred_element_type=f32)
    o_ref[...] = x_ref[...] + _rms(y, g_ref[...])


def _mix_out(a1, a2, w, g, x):
    t, d = x.shape
    k1, k2 = a1.shape[1], a2.shape[1]
    assert k1 == k2 and w.shape[0] == k1 + k2
    tm = min(OUT_TM, t)
    assert t % tm == 0
    return pl.pallas_call(
        _mix_out_kernel,
        grid=(t // tm,),
        in_specs=[
            pl.BlockSpec((tm, k1), lambda i: (i, 0)),
            pl.BlockSpec((tm, k2), lambda i: (i, 0)),
            pl.BlockSpec((k1, d), lambda i: (0, 0)),
            pl.BlockSpec((k2, d), lambda i: (1, 0)),
            pl.BlockSpec((1, d), lambda i: (0, 0)),
            pl.BlockSpec((tm, d), lambda i: (i, 0)),
        ],
        out_specs=pl.BlockSpec((tm, d), lambda i: (i, 0)),
        out_shape=jax.ShapeDtypeStruct((t, d), f32),
        compiler_params=_params(("parallel",)),
        name="mix_out",
    )(a1, a2, w, w, g.reshape(1, d), x)


def _t5_causal_bucket(dist):
    n = jnp.maximum(dist, 0)
    max_exact = REL_BUCKETS // 2
    nf = jnp.maximum(n, 1).astype(f32)
    large = max_exact + (jnp.log(nf / max_exact) / math.log(REL_MAX_DIST / max_exact)
                         * (REL_BUCKETS - max_exact)).astype(jnp.int32)
    large = jnp.minimum(large, REL_BUCKETS - 1)
    return jnp.where(n < max_exact, n, large)


def _toeplitz(vec, t):
    h = vec.shape[0]
    r = jnp.concatenate([vec[:, ::-1], jnp.zeros((h, 1), vec.dtype)], axis=1)
    g = jnp.tile(r, (1, t))[:, :t * (2 * t - 1)].reshape(h, t, 2 * t - 1)
    return g[:, :, t - 1:]


def _da_bias_tiles(rel_bias, tq):
    d = jnp.arange(-(tq - 1), tq)
    b0 = jnp.where((d >= 0)[:, None], rel_bias[_t5_causal_bucket(d)].astype(f32), MASK_VALUE)
    b1 = rel_bias[_t5_causal_bucket(d + tq)].astype(f32)
    return jnp.stack([_toeplitz(b0.T, tq), _toeplitz(b1.T, tq)], axis=1)


def _da_kernel(q_ref, k_ref, v_ref, bias_ref, far_ref, lq1_ref, lk1_ref, lq2_ref, lk2_ref, g_ref,
               o_ref, m_ref, l_ref, acc_ref, *, tq, lam_init):
    i = pl.program_id(2)
    q = q_ref[...]
    lane = lax.broadcasted_iota(jnp.int32, q.shape, 1)
    scale = jnp.asarray(DA_QK_DIM ** -0.5, q.dtype)
    zero = jnp.zeros_like(q)
    q_maps = (jnp.where(lane < DA_QK_DIM, q * scale, zero), jnp.where(lane >= DA_QK_DIM, q * scale, zero))

    m_ref[...] = jnp.full(m_ref.shape, MASK_VALUE, f32)
    l_ref[...] = jnp.zeros(l_ref.shape, f32)
    acc_ref[...] = jnp.zeros(acc_ref.shape, f32)

    def step(j, bias):
        rows = pl.ds(pl.multiple_of(j * tq, tq), tq)
        kb = k_ref[rows, :]
        vb = v_ref[rows, :]
        for mi in range(2):
            s = lax.dot_general(q_maps[mi], kb, _NT, preferred_element_type=f32) + bias
            m_prev = m_ref[mi]
            m_new = jnp.maximum(m_prev, jnp.max(s, axis=-1, keepdims=True))
            alpha = jnp.exp(m_prev - m_new)
            p = jnp.exp(s - jnp.tile(m_new, (1, tq // LANES)))
            l_ref[mi] = alpha * l_ref[mi] + jnp.sum(p, axis=-1, keepdims=True)
            acc_ref[mi] = alpha * acc_ref[mi] + jnp.dot(p.astype(vb.dtype), vb, preferred_element_type=f32)
            m_ref[mi] = m_new

    def far_body(j, carry):
        step(j, far_ref[...])
        return carry

    lax.fori_loop(0, jnp.maximum(i - 1, 0), far_body, 0)

    @pl.when(i >= 1)
    def _():
        step(i - 1, bias_ref[1])

    step(i, bias_ref[0])

    lam = (jnp.exp(jnp.sum(lq1_ref[...] * lk1_ref[...], axis=-1, keepdims=True))
           - jnp.exp(jnp.sum(lq2_ref[...] * lk2_ref[...], axis=-1, keepdims=True)) + lam_init)
    o = acc_ref[0] / l_ref[0] - lam * (acc_ref[1] / l_ref[1])
    o_ref[...] = (_rms(o, g_ref[...]) * (1.0 - lam_init)).astype(o_ref.dtype)


def _diff_attention(proj, rel_bias, lq1, lk1, lq2, lk2, subln_g, *, batch, seq, lam_init):
    tq = min(DA_TQ, seq)
    assert seq % tq == 0 and tq >= REL_MAX_DIST
    nq = seq // tq
    bias = _da_bias_tiles(rel_bias, tq)
    far = jnp.broadcast_to(rel_bias[REL_BUCKETS - 1].astype(f32)[:, None, None], (DA_HEADS, 1, tq))
    vec = lambda a: a.reshape(1, DA_QK_DIM).astype(f32)
    lam_spec = pl.BlockSpec((1, DA_QK_DIM), lambda b, h, i: (0, 0))
    return pl.pallas_call(
        functools.partial(_da_kernel, tq=tq, lam_init=lam_init),
        grid=(batch, DA_HEADS, nq),
        in_specs=[
            pl.BlockSpec((tq, DA_V_DIM), lambda b, h, i: (b * nq + i, h)),
            pl.BlockSpec((seq, DA_V_DIM), lambda b, h, i: (b, DA_HEADS + h)),
            pl.BlockSpec((seq, DA_V_DIM), lambda b, h, i: (b, 2 * DA_HEADS + h)),
            pl.BlockSpec((None, 2, tq, tq), lambda b, h, i: (h, 0, 0, 0)),
            pl.BlockSpec((None, 1, tq), lambda b, h, i: (h, 0, 0)),
            lam_spec, lam_spec, lam_spec, lam_spec,
            pl.BlockSpec((1, DA_V_DIM), lambda b, h, i: (0, 0)),
        ],
        out_specs=pl.BlockSpec((tq, DA_V_DIM), lambda b, h, i: (b * nq + i, h)),
        out_shape=jax.ShapeDtypeStruct((batch * seq, DA_WIDTH), bf16),
        scratch_shapes=[
            pltpu.VMEM((2, tq, LANES), f32),
            pltpu.VMEM((2, tq, LANES), f32),
            pltpu.VMEM((2, tq, DA_V_DIM), f32),
        ],
        compiler_params=_params(("parallel", "parallel", "arbitrary")),
        name="diff_attention",
    )(proj, proj, proj, bias, far, vec(lq1), vec(lk1), vec(lq2), vec(lk2), subln_g.reshape(1, DA_V_DIM))


def _hg_kernel(f_ref, q_ref, i_ref, g_ref, lbl_ref, on_ref, o_ref,
               st_ref, k_ref, b_ref, oacc_ref, *, layer, tc, sb):
    @pl.when(pl.program_id(2) == 0)
    def _():
        st_ref[...] = jnp.zeros(st_ref.shape, f32)

    lg = lbl_ref[...]
    pe = jnp.exp(lg - jnp.max(lg, axis=0, keepdims=True))
    lb = jnp.zeros((1, lg.shape[1]), f32)
    for r in range(1, layer + 1):
        lb = lb + pe[r:r + 1]
    lb = lb / jnp.sum(pe, axis=0, keepdims=True)

    f = lb + (1.0 - lb) * _sigmoid(f_ref[...])
    k_ref[...] = 1.0 - f
    b = jnp.log(f)
    r = lax.broadcasted_iota(jnp.int32, b.shape, 0) % sb
    sh = 1
    while sh < sb:
        b = b + jnp.where(r >= sh, pltpu.roll(b, sh, axis=0), 0.0)
        sh *= 2
    b_ref[...] = b

    trow = lax.broadcasted_iota(jnp.int32, (SUBLANES, 1), 0)

    def body(n, carry):
        rows = pl.ds(pl.multiple_of(n * sb, sb), sb)
        qb, kb, bb, vb = q_ref[rows, :], k_ref[rows, :], b_ref[rows, :], i_ref[rows, :]
        be = bb[sb - 1:sb, :]
        st = st_ref[...]
        qd = qb * jnp.exp(bb)
        kd = kb * jnp.exp(be - bb)
        o = lax.dot_general(qd.astype(bf16), st.astype(bf16), _NT, preferred_element_type=f32)
        pieces = [slice(SUBLANES * p, SUBLANES * (p + 1)) for p in range(sb // SUBLANES)]
        op = [o[p] for p in pieces]
        for s in range(sb):
            ks, bs, vs = kb[s:s + 1, :], bb[s:s + 1, :], vb[s:s + 1, :]
            for pi in range(s // SUBLANES, len(pieces)):
                d = bb[pieces[pi]] - bs
                on_diagonal = s >= SUBLANES * pi
                if on_diagonal:
                    d = jnp.minimum(d, 0.0)
                a = jnp.sum(qb[pieces[pi]] * ks * jnp.exp(d), axis=-1, keepdims=True)
                if on_diagonal:
                    a = jnp.where(trow >= s - SUBLANES * pi, a, 0.0)
                op[pi] = op[pi] + a * vs
        oacc_ref[rows, :] = jnp.concatenate(op, axis=0)
        st_ref[...] = jnp.exp(be) * st + lax.dot_general(
            vb.astype(bf16), kd.astype(bf16), _TN, preferred_element_type=f32)
        return carry

    lax.fori_loop(0, tc // sb, body, 0, unroll=HG_UNROLL)

    gate = g_ref[...]
    o_ref[...] = (_rms(oacc_ref[...], on_ref[...]) * (gate * _sigmoid(gate))).astype(o_ref.dtype)


def _hgrn2(proj, lb_logits, onorm_g, *, batch, seq, layer):
    tc = min(HG_TC, seq)
    assert seq % tc == 0 and tc % HG_SB == 0
    nc = seq // tc
    depth = lb_logits.shape[0]
    col = lambda part: pl.BlockSpec((tc, HG_DK), lambda b, h, c: (b * nc + c, part * HG_HEADS + h))
    return pl.pallas_call(
        functools.partial(_hg_kernel, layer=layer, tc=tc, sb=HG_SB),
        grid=(batch, HG_HEADS, nc),
        in_specs=[
            col(0), col(1), col(2), col(3),
            pl.BlockSpec((depth, HG_DK), lambda b, h, c: (0, h)),
            pl.BlockSpec((1, HG_DV), lambda b, h, c: (0, 0)),
        ],
        out_specs=pl.BlockSpec((tc, HG_DV), lambda b, h, c: (b * nc + c, h)),
        out_shape=jax.ShapeDtypeStruct((batch * seq, HG_WIDTH), bf16),
        scratch_shapes=[
            pltpu.VMEM((HG_DV, HG_DK), f32),
            pltpu.VMEM((tc, HG_DK), f32),
            pltpu.VMEM((tc, HG_DK), f32),
            pltpu.VMEM((tc, HG_DV), f32),
        ],
        compiler_params=_params(("parallel", "parallel", "arbitrary")),
        name="hgrn2",
    )(proj, proj, proj, proj, lb_logits.astype(f32), onorm_g.reshape(1, HG_DV))


def _cross_kernel(x_ref, gpre_ref, wq_ref, kv_ref, wo_ref, gpost_ref, o_ref):
    x = x_ref[...]
    h = _rms(x, gpre_ref[...]).astype(bf16)
    q = jnp.dot(h, wq_ref[...], preferred_element_type=f32).astype(bf16)
    outs = []
    for hd in range(CX_HEADS):
        cols = slice(hd * CX_HEAD_DIM, (hd + 1) * CX_HEAD_DIM)
        kh = kv_ref[:, cols]
        vh = kv_ref[:, CX_WIDTH + hd * CX_HEAD_DIM:CX_WIDTH + (hd + 1) * CX_HEAD_DIM]
        s = lax.dot_general(q[:, cols], kh, _NT, preferred_element_type=f32) * CX_HEAD_DIM ** -0.5
        p = jnp.exp(s - jnp.max(s, axis=-1, keepdims=True))
        oh = jnp.dot(p.astype(bf16), vh, preferred_element_type=f32)
        outs.append((oh / jnp.sum(p, axis=-1, keepdims=True)).astype(bf16))
    o = jnp.concatenate(outs, axis=-1)
    y = jnp.dot(o, wo_ref[...], preferred_element_type=f32)
    o_ref[...] = x + _rms(y, gpost_ref[...])


def _cross_attention(x, kv, wq, wo, gpre, gpost, *, batch, seq, mem_len):
    t, d = x.shape
    tm = min(CX_TM, seq)
    assert seq % tm == 0
    per_batch = seq // tm
    return pl.pallas_call(
        _cross_kernel,
        grid=(t // tm,),
        in_specs=[
            pl.BlockSpec((tm, d), lambda i: (i, 0)),
            pl.BlockSpec((1, d), lambda i: (0, 0)),
            pl.BlockSpec((d, CX_WIDTH), lambda i: (0, 0)),
            pl.BlockSpec((mem_len, 2 * CX_WIDTH), lambda i: (i // per_batch, 0)),
            pl.BlockSpec((CX_WIDTH, d), lambda i: (0, 0)),
            pl.BlockSpec((1, d), lambda i: (0, 0)),
        ],
        out_specs=pl.BlockSpec((tm, d), lambda i: (i, 0)),
        out_shape=jax.ShapeDtypeStruct((t, d), f32),
        compiler_params=_params(("parallel",)),
        name="cross_attention",
    )(x, gpre.reshape(1, d), wq, kv, wo, gpost.reshape(1, d))


def _ffn_kernel(x_ref, gpre_ref, wg_ref, wu_ref, wo_ref, gpost_ref, o_ref, h_ref, acc_ref):
    j = pl.program_id(1)

    @pl.when(j == 0)
    def _():
        h_ref[...] = _rms(x_ref[...], gpre_ref[...]).astype(bf16)
        acc_ref[...] = jnp.zeros(acc_ref.shape, f32)

    h = h_ref[...]
    gate = jnp.dot(h, wg_ref[...], preferred_element_type=f32)
    up = jnp.dot(h, wu_ref[...], preferred_element_type=f32)
    a = (gate * _sigmoid(gate) * up).astype(bf16)
    acc_ref[...] += jnp.dot(a, wo_ref[...], preferred_element_type=f32)

    @pl.when(j == pl.num_programs(1) - 1)
    def _():
        o_ref[...] = x_ref[...] + _rms(acc_ref[...], gpost_ref[...])


def _ffn(x, w_in, w_out, gpre, gpost):
    t, d = x.shape
    hidden = w_out.shape[0]
    tm, th = min(FFN_TM, t), min(FFN_TH, hidden)
    assert t % tm == 0 and hidden % th == 0 and w_in.shape[1] == 2 * hidden
    nh = hidden // th
    return pl.pallas_call(
        _ffn_kernel,
        grid=(t // tm, nh),
        in_specs=[
            pl.BlockSpec((tm, d), lambda i, j: (i, 0)),
            pl.BlockSpec((1, d), lambda i, j: (0, 0)),
            pl.BlockSpec((d, th), lambda i, j: (0, j)),
            pl.BlockSpec((d, th), lambda i, j: (0, nh + j)),
            pl.BlockSpec((th, d), lambda i, j: (j, 0)),
            pl.BlockSpec((1, d), lambda i, j: (0, 0)),
        ],
        out_specs=pl.BlockSpec((tm, d), lambda i, j: (i, 0)),
        out_shape=jax.ShapeDtypeStruct((t, d), f32),
        scratch_shapes=[pltpu.VMEM((tm, d), bf16), pltpu.VMEM((tm, d), f32)],
        compiler_params=_params(("parallel", "arbitrary")),
        name="ffn",
    )(x, gpre.reshape(1, d), w_in, w_in, w_out, gpost.reshape(1, d))


def kernel(x, mem, w_in, w_out, w_cq, w_ckv, w_co, w_ffn_in, w_ffn_out, mix_pre_g, mix_post_g, cross_pre_g, cross_post_g, mem_norm_g, ffn_pre_g, ffn_post_g, da_subln_g, hg_onorm_g, lambda_q1, lambda_k1, lambda_q2, lambda_k2, hg_lb_logits, rel_bias):
    batch, seq, d = x.shape
    mem_len = mem.shape[1]
    depth = w_in.shape[0]
    da_cols = 3 * DA_WIDTH
    xt = x.reshape(batch * seq, d)
    memt = mem.reshape(batch * mem_len, d)

    for l in range(depth):
        lam_init = 0.8 - 0.6 * math.exp(-0.3 * l)
        proj_da = _norm_matmul(xt, mix_pre_g[l], w_in[l, :, :da_cols].astype(bf16), bf16)
        proj_hg = _norm_matmul(xt, mix_pre_g[l], w_in[l, :, da_cols:].astype(bf16), f32)
        o_da = _diff_attention(proj_da, rel_bias, lambda_q1[l], lambda_k1[l], lambda_q2[l], lambda_k2[l],
                               da_subln_g[l], batch=batch, seq=seq, lam_init=lam_init)
        o_hg = _hgrn2(proj_hg, hg_lb_logits, hg_onorm_g[l], batch=batch, seq=seq, layer=l)
        xt = _mix_out(o_da, o_hg, w_out[l].astype(bf16), mix_post_g[l], xt)
        kv = _norm_matmul(memt, mem_norm_g[l], w_ckv[l].astype(bf16), bf16)
        xt = _cross_attention(xt, kv, w_cq[l].astype(bf16), w_co[l].astype(bf16),
                              cross_pre_g[l], cross_post_g[l], batch=batch, seq=seq, mem_len=mem_len)
        xt = _ffn(xt, w_ffn_in[l].astype(bf16), w_ffn_out[l].astype(bf16), ffn_pre_g[l], ffn_post_g[l])

    return xt.reshape(batch, seq, d)
```

```python
import functools
import math

import jax
import jax.numpy as jnp
from jax import lax
from jax.experimental import pallas as pl
from jax.experimental.pallas import tpu as pltpu

NORM_EPS = 1e-6

DA_HEADS = 8
DA_QK_DIM = 64
DA_V_DIM = 2 * DA_QK_DIM
DA_WIDTH = DA_HEADS * DA_V_DIM
HG_HEADS = 8
HG_DK = 128
HG_DV = 128
HG_WIDTH = HG_HEADS * HG_DV
REL_BUCKETS = 32
REL_MAX_DIST = 128
CX_HEADS = 4
CX_HEAD_DIM = 128
CX_WIDTH = CX_HEADS * CX_HEAD_DIM

LANES = 128
SUBLANES = 8
MASK_VALUE = -1e30
LOG2E = math.log2(math.e)
VMEM_LIMIT = 56 * 1024 * 1024

MM_TM = 1024
MM_TN = 512
OUT_TM = 512
CX_TM = 512
FFN_TM = 512
FFN_TH = 512
DA_TQ = 512
HG_TC = 512
HG_SB = 16

_NT = (((1,), (1,)), ((), ()))
_TN = (((0,), (0,)), ((), ()))

f32 = jnp.float32
bf16 = jnp.bfloat16


def _rms(x, g):
    return x * lax.rsqrt(jnp.mean(x * x, axis=-1, keepdims=True) + NORM_EPS) * g


def _sigmoid(x):
    return 1.0 / (1.0 + jnp.exp(-x))


def _params(sem):
    return pltpu.CompilerParams(dimension_semantics=sem, vmem_limit_bytes=VMEM_LIMIT)


def _norm_matmul_kernel(x_ref, g_ref, w_ref, o_ref, h_ref, *, lead_blocks, lead_scale):
    @pl.when(pl.program_id(1) == 0)
    def _():
        h_ref[...] = _rms(x_ref[...], g_ref[...]).astype(bf16)

    y = jnp.dot(h_ref[...], w_ref[...], preferred_element_type=f32)
    if lead_blocks:
        y = y * jnp.where(pl.program_id(1) < lead_blocks, lead_scale, 1.0)
    o_ref[...] = y.astype(o_ref.dtype)


def _norm_matmul(x, g, w, out_dtype, lead_cols=0, lead_scale=1.0):
    t, d = x.shape
    n = w.shape[1]
    tm, tn = min(MM_TM, t), min(MM_TN, n)
    assert t % tm == 0 and n % tn == 0 and lead_cols % tn == 0
    return pl.pallas_call(
        functools.partial(_norm_matmul_kernel, lead_blocks=lead_cols // tn, lead_scale=lead_scale),
        grid=(t // tm, n // tn),
        in_specs=[
            pl.BlockSpec((tm, d), lambda i, j: (i, 0)),
            pl.BlockSpec((1, d), lambda i, j: (0, 0)),
            pl.BlockSpec((d, tn), lambda i, j: (0, j)),
        ],
        out_specs=pl.BlockSpec((tm, tn), lambda i, j: (i, j)),
        out_shape=jax.ShapeDtypeStruct((t, n), out_dtype),
        scratch_shapes=[pltpu.VMEM((tm, d), bf16)],
        compiler_params=_params(("parallel", "arbitrary")),
        name="norm_matmul",
    )(x, g.reshape(1, d), w)


def _mix_out_kernel(a1_ref, a2_ref, w1_ref, w2_ref, g_ref, x_ref, o_ref):
    y = jnp.dot(a1_ref[...], w1_ref[...], preferred_element_type=f32)
    y = y + jnp.dot(a2_ref[...], w2_ref[...], preferred_element_type=f32)
    o_ref[...] = x_ref[...] + _rms(y, g_ref[...])


def _mix_out(a1, a2, w, g, x):
    t, d = x.shape
    k1, k2 = a1.shape[1], a2.shape[1]
    assert k1 == k2 and w.shape[0] == k1 + k2
    tm = min(OUT_TM, t)
    assert t % tm == 0
    return pl.pallas_call(
        _mix_out_kernel,
        grid=(t // tm,),
        in_specs=[
            pl.BlockSpec((tm, k1), lambda i: (i, 0)),
            pl.BlockSpec((tm, k2), lambda i: (i, 0)),
            pl.BlockSpec((k1, d), lambda i: (0, 0)),
            pl.BlockSpec((k2, d), lambda i: (1, 0)),
            pl.BlockSpec((1, d), lambda i: (0, 0)),
            pl.BlockSpec((tm, d), lambda i: (i, 0)),
        ],
        out_specs=pl.BlockSpec((tm, d), lambda i: (i, 0)),
        out_shape=jax.ShapeDtypeStruct((t, d), f32),
        compiler_params=_params(("parallel",)),
        name="mix_out",
    )(a1, a2, w, w, g.reshape(1, d), x)


def _t5_causal_bucket(dist):
    n = jnp.maximum(dist, 0)
    max_exact = REL_BUCKETS // 2
    nf = jnp.maximum(n, 1).astype(f32)
    large = max_exact + (jnp.log(nf / max_exact) / math.log(REL_MAX_DIST / max_exact)
                         * (REL_BUCKETS - max_exact)).astype(jnp.int32)
    large = jnp.minimum(large, REL_BUCKETS - 1)
    return jnp.where(n < max_exact, n, large)


def _toeplitz(vec, t):
    h = vec.shape[0]
    r = jnp.concatenate([vec[:, ::-1], jnp.zeros((h, 1), vec.dtype)], axis=1)
    g = jnp.tile(r, (1, t))[:, :t * (2 * t - 1)].reshape(h, t, 2 * t - 1)
    return g[:, :, t - 1:]


def _da_bias_tiles(rel_bias, tq):
    d = jnp.arange(-(tq - 1), tq)
    b0 = jnp.where((d >= 0)[:, None], rel_bias[_t5_causal_bucket(d)].astype(f32), MASK_VALUE)
    b1 = rel_bias[_t5_causal_bucket(d + tq)].astype(f32)
    return jnp.stack([_toeplitz(b0.T, tq), _toeplitz(b1.T, tq)], axis=1)


def _da_kernel(q_ref, k_ref, v_ref, bias_ref, far_ref, lq1_ref, lk1_ref, lq2_ref, lk2_ref, g_ref,
               o_ref, m_ref, l_ref, acc_ref, *, tq, lam_init):
    i = pl.program_id(2)
    q = q_ref[...]
    lane = lax.broadcasted_iota(jnp.int32, q.shape, 1)
    zero = jnp.zeros_like(q)
    q_maps = (jnp.where(lane < DA_QK_DIM, q, zero), jnp.where(lane >= DA_QK_DIM, q, zero))

    m_ref[...] = jnp.full(m_ref.shape, MASK_VALUE, f32)
    l_ref[...] = jnp.zeros(l_ref.shape, f32)
    acc_ref[...] = jnp.zeros(acc_ref.shape, f32)

    def step(j, bias):
        rows = pl.ds(pl.multiple_of(j * tq, tq), tq)
        kb = k_ref[rows, :]
        vb = v_ref[rows, :]
        for mi in range(2):
            s = lax.dot_general(q_maps[mi], kb, _NT, preferred_element_type=f32)
            if bias is not None:
                s = s + bias
            m_prev = m_ref[mi]
            m_new = jnp.maximum(m_prev, jnp.max(s, axis=-1, keepdims=True))
            alpha = jnp.exp2(m_prev - m_new)
            p = jnp.exp2(s - jnp.tile(m_new, (1, tq // LANES)))
            l_ref[mi] = alpha * l_ref[mi] + jnp.sum(p, axis=-1, keepdims=True)
            acc_ref[mi] = alpha * acc_ref[mi] + jnp.dot(p.astype(vb.dtype), vb, preferred_element_type=f32)
            m_ref[mi] = m_new

    n_far = jnp.maximum(i - 1, 0)

    def far_pair(t, carry):
        step(2 * t, None)
        step(2 * t + 1, None)
        return carry

    lax.fori_loop(0, n_far // 2, far_pair, 0)

    @pl.when(n_far % 2 == 1)
    def _():
        step(n_far - 1, None)

    m_ref[...] = m_ref[...] + far_ref[...]

    @pl.when(i >= 1)
    def _():
        step(i - 1, bias_ref[1])

    step(i, bias_ref[0])

    lam = (jnp.exp(jnp.sum(lq1_ref[...] * lk1_ref[...], axis=-1, keepdims=True))
           - jnp.exp(jnp.sum(lq2_ref[...] * lk2_ref[...], axis=-1, keepdims=True)) + lam_init)
    o = acc_ref[0] / l_ref[0] - lam * (acc_ref[1] / l_ref[1])
    o_ref[...] = (_rms(o, g_ref[...]) * (1.0 - lam_init)).astype(o_ref.dtype)


def _diff_attention(proj, rel_bias, lq1, lk1, lq2, lk2, subln_g, *, batch, seq, lam_init):
    tq = min(DA_TQ, seq)
    assert seq % tq == 0 and tq >= REL_MAX_DIST
    nq = seq // tq
    bias = _da_bias_tiles(rel_bias * LOG2E, tq)
    far = jnp.broadcast_to((rel_bias[REL_BUCKETS - 1].astype(f32) * LOG2E)[:, None, None], (DA_HEADS, 1, LANES))
    vec = lambda a: a.reshape(1, DA_QK_DIM).astype(f32)
    lam_spec = pl.BlockSpec((1, DA_QK_DIM), lambda b, h, i: (0, 0))
    return pl.pallas_call(
        functools.partial(_da_kernel, tq=tq, lam_init=lam_init),
        grid=(batch, DA_HEADS, nq),
        in_specs=[
            pl.BlockSpec((tq, DA_V_DIM), lambda b, h, i: (b * nq + i, h)),
            pl.BlockSpec((seq, DA_V_DIM), lambda b, h, i: (b, DA_HEADS + h)),
            pl.BlockSpec((seq, DA_V_DIM), lambda b, h, i: (b, 2 * DA_HEADS + h)),
            pl.BlockSpec((None, 2, tq, tq), lambda b, h, i: (h, 0, 0, 0)),
            pl.BlockSpec((None, 1, LANES), lambda b, h, i: (h, 0, 0)),
            lam_spec, lam_spec, lam_spec, lam_spec,
            pl.BlockSpec((1, DA_V_DIM), lambda b, h, i: (0, 0)),
        ],
        out_specs=pl.BlockSpec((tq, DA_V_DIM), lambda b, h, i: (b * nq + i, h)),
        out_shape=jax.ShapeDtypeStruct((batch * seq, DA_WIDTH), bf16),
        scratch_shapes=[
            pltpu.VMEM((2, tq, LANES), f32),
            pltpu.VMEM((2, tq, LANES), f32),
            pltpu.VMEM((2, tq, DA_V_DIM), f32),
        ],
        compiler_params=_params(("parallel", "parallel", "arbitrary")),
        name="diff_attention",
    )(proj, proj, proj, bias, far, vec(lq1), vec(lk1), vec(lq2), vec(lk2), subln_g.reshape(1, DA_V_DIM))


def _hg_kernel(f_ref, q_ref, i_ref, g_ref, lbl_ref, on_ref, o_ref,
               st_ref, kf_ref, bf_ref, vf_ref, *, layer, tc, sb):
    groups = tc // sb

    @pl.when(pl.program_id(2) == 0)
    def _():
        st_ref[...] = jnp.zeros(st_ref.shape, f32)

    lg = lbl_ref[...]
    pe = jnp.exp(lg - jnp.max(lg, axis=0, keepdims=True))
    lb = jnp.zeros((1, lg.shape[1]), f32)
    for r in range(1, layer + 1):
        lb = lb + pe[r:r + 1]
    lb = lb / jnp.sum(pe, axis=0, keepdims=True)

    f = lb + (1.0 - lb) * _sigmoid(f_ref[...].astype(f32))
    k = 1.0 - f
    b = jnp.log2(f)
    r = lax.broadcasted_iota(jnp.int32, b.shape, 0) % sb
    sh = 1
    while sh < sb:
        b = b + jnp.where(r >= sh, pltpu.roll(b, sh, axis=0), 0.0)
        sh *= 2
    q = q_ref[...].astype(f32)
    v = i_ref[...].astype(f32)
    kf_ref[...] = k
    bf_ref[...] = b
    vf_ref[...] = v

    def source_row(ref, s):
        return jnp.stack([jnp.broadcast_to(ref[sb * g + s:sb * g + s + 1, :], (SUBLANES, ref.shape[1]))
                          for g in range(groups)])

    def pieces(x):
        lo = jnp.stack([x[sb * g:sb * g + SUBLANES] for g in range(groups)])
        hi = jnp.stack([x[sb * g + SUBLANES:sb * (g + 1)] for g in range(groups)])
        return lo, hi

    q_lo, q_hi = pieces(q)
    k_lo, k_hi = pieces(k)
    b_lo, b_hi = pieces(b)
    v_lo, v_hi = pieces(v)
    b_end = b_hi[:, SUBLANES - 1:SUBLANES, :]

    qd_lo, qd_hi = q_lo * jnp.exp2(b_lo), q_hi * jnp.exp2(b_hi)
    kd_lo, kd_hi = k_lo * jnp.exp2(b_end - b_lo), k_hi * jnp.exp2(b_end - b_hi)
    block_decay = jnp.exp2(b_end[:, 0, :])
    vb = v.astype(bf16)
    increments = []
    for g in range(groups):
        kd = jnp.concatenate([kd_lo[g], kd_hi[g]], axis=0).astype(bf16)
        increments.append(lax.dot_general(vb[sb * g:sb * (g + 1)], kd, _TN, preferred_element_type=f32))
    st = st_ref[...]
    states = []
    for g in range(groups):
        states.append(st.astype(bf16))
        st = block_decay[g:g + 1] * st + increments[g]
    st_ref[...] = st
    o_state = []
    for g in range(groups):
        qd = jnp.concatenate([qd_lo[g], qd_hi[g]], axis=0).astype(bf16)
        o_state.append(lax.dot_general(qd, states[g], _NT, preferred_element_type=f32))

    trow = lax.broadcasted_iota(jnp.int32, (1, SUBLANES, 1), 1)

    def contrib(q_p, b_p, ks, bs, vs, diag_row):
        d = b_p - bs
        if diag_row is not None:
            d = jnp.minimum(d, 0.0)
        a = jnp.sum(q_p * ks * jnp.exp2(d), axis=-1, keepdims=True)
        if diag_row is not None:
            a = jnp.where(trow >= diag_row, a, 0.0)
        return a * vs

    o_lo = jnp.zeros(q_lo.shape, f32)
    o_hi = jnp.zeros(q_hi.shape, f32)
    for s in range(SUBLANES):
        ks, bs, vs = source_row(kf_ref, s), source_row(bf_ref, s), source_row(vf_ref, s)
        o_lo = o_lo + contrib(q_lo, b_lo, ks, bs, vs, s)
        o_hi = o_hi + contrib(q_hi, b_hi, ks, bs, vs, None)
    for s in range(SUBLANES):
        ks, bs, vs = (source_row(kf_ref, SUBLANES + s), source_row(bf_ref, SUBLANES + s),
                      source_row(vf_ref, SUBLANES + s))
        o_hi = o_hi + contrib(q_hi, b_hi, ks, bs, vs, s)
    o = jnp.concatenate(
        [jnp.concatenate([o_lo[g], o_hi[g]], axis=0) + o_state[g] for g in range(groups)], axis=0)

    gate = g_ref[...].astype(f32)
    o_ref[...] = (_rms(o, on_ref[...]) * (gate * _sigmoid(gate))).astype(o_ref.dtype)


def _hgrn2(proj, lb_logits, onorm_g, *, batch, seq, layer):
    tc = min(HG_TC, seq)
    assert seq % tc == 0 and tc % HG_SB == 0 and HG_SB == 2 * SUBLANES
    nc = seq // tc
    depth = lb_logits.shape[0]
    first = (proj.shape[1] - 4 * HG_WIDTH) // HG_DK
    col = lambda part: pl.BlockSpec((tc, HG_DK), lambda b, h, c: (b * nc + c, first + part * HG_HEADS + h))
    return pl.pallas_call(
        functools.partial(_hg_kernel, layer=layer, tc=tc, sb=HG_SB),
        grid=(batch, HG_HEADS, nc),
        in_specs=[
            col(0), col(1), col(2), col(3),
            pl.BlockSpec((depth, HG_DK), lambda b, h, c: (0, h)),
            pl.BlockSpec((1, HG_DV), lambda b, h, c: (0, 0)),
        ],
        out_specs=pl.BlockSpec((tc, HG_DV), lambda b, h, c: (b * nc + c, h)),
        out_shape=jax.ShapeDtypeStruct((batch * seq, HG_WIDTH), bf16),
        scratch_shapes=[pltpu.VMEM((HG_DV, HG_DK), f32)] + [pltpu.VMEM((tc, HG_DK), f32)] * 3,
        compiler_params=_params(("parallel", "parallel", "arbitrary")),
        name="hgrn2",
    )(proj, proj, proj, proj, lb_logits.astype(f32), onorm_g.reshape(1, HG_DV))


def _cross_kernel(x_ref, gpre_ref, wq_ref, kv_ref, wo_ref, gpost_ref, o_ref):
    x = x_ref[...]
    h = _rms(x, gpre_ref[...]).astype(bf16)
    q = jnp.dot(h, wq_ref[...], preferred_element_type=f32).astype(bf16)
    outs = []
    for hd in range(CX_HEADS):
        cols = slice(hd * CX_HEAD_DIM, (hd + 1) * CX_HEAD_DIM)
        kh = kv_ref[:, cols]
        vh = kv_ref[:, CX_WIDTH + hd * CX_HEAD_DIM:CX_WIDTH + (hd + 1) * CX_HEAD_DIM]
        s = lax.dot_general(q[:, cols], kh, _NT, preferred_element_type=f32) * CX_HEAD_DIM ** -0.5
        p = jnp.exp(s - jnp.max(s, axis=-1, keepdims=True))
        oh = jnp.dot(p.astype(bf16), vh, preferred_element_type=f32)
        outs.append((oh / jnp.sum(p, axis=-1, keepdims=True)).astype(bf16))
    o = jnp.concatenate(outs, axis=-1)
    y = jnp.dot(o, wo_ref[...], preferred_element_type=f32)
    o_ref[...] = x + _rms(y, gpost_ref[...])


def _cross_attention(x, kv, wq, wo, gpre, gpost, *, batch, seq, mem_len):
    t, d = x.shape
    tm = min(CX_TM, seq)
    assert seq % tm == 0
    per_batch = seq // tm
    return pl.pallas_call(
        _cross_kernel,
        grid=(t // tm,),
        in_specs=[
            pl.BlockSpec((tm, d), lambda i: (i, 0)),
            pl.BlockSpec((1, d), lambda i: (0, 0)),
            pl.BlockSpec((d, CX_WIDTH), lambda i: (0, 0)),
            pl.BlockSpec((mem_len, 2 * CX_WIDTH), lambda i: (i // per_batch, 0)),
            pl.BlockSpec((CX_WIDTH, d), lambda i: (0, 0)),
            pl.BlockSpec((1, d), lambda i: (0, 0)),
        ],
        out_specs=pl.BlockSpec((tm, d), lambda i: (i, 0)),
        out_shape=jax.ShapeDtypeStruct((t, d), f32),
        compiler_params=_params(("parallel",)),
        name="cross_attention",
    )(x, gpre.reshape(1, d), wq, kv, wo, gpost.reshape(1, d))


def _ffn_kernel(x_ref, gpre_ref, wg_ref, wu_ref, wo_ref, gpost_ref, o_ref, h_ref, acc_ref):
    j = pl.program_id(1)

    @pl.when(j == 0)
    def _():
        h_ref[...] = _rms(x_ref[...], gpre_ref[...]).astype(bf16)
        acc_ref[...] = jnp.zeros(acc_ref.shape, f32)

    h = h_ref[...]
    gate = jnp.dot(h, wg_ref[...], preferred_element_type=f32)
    up = jnp.dot(h, wu_ref[...], preferred_element_type=f32)
    a = (gate * _sigmoid(gate) * up).astype(bf16)
    acc_ref[...] += jnp.dot(a, wo_ref[...], preferred_element_type=f32)

    @pl.when(j == pl.num_programs(1) - 1)
    def _():
        o_ref[...] = x_ref[...] + _rms(acc_ref[...], gpost_ref[...])


def _ffn(x, w_in, w_out, gpre, gpost):
    t, d = x.shape
    hidden = w_out.shape[0]
    tm, th = min(FFN_TM, t), min(FFN_TH, hidden)
    assert t % tm == 0 and hidden % th == 0 and w_in.shape[1] == 2 * hidden
    nh = hidden // th
    return pl.pallas_call(
        _ffn_kernel,
        grid=(t // tm, nh),
        in_specs=[
            pl.BlockSpec((tm, d), lambda i, j: (i, 0)),
            pl.BlockSpec((1, d), lambda i, j: (0, 0)),
            pl.BlockSpec((d, th), lambda i, j: (0, j)),
            pl.BlockSpec((d, th), lambda i, j: (0, nh + j)),
            pl.BlockSpec((th, d), lambda i, j: (j, 0)),
            pl.BlockSpec((1, d), lambda i, j: (0, 0)),
        ],
        out_specs=pl.BlockSpec((tm, d), lambda i, j: (i, 0)),
        out_shape=jax.ShapeDtypeStruct((t, d), f32),
        scratch_shapes=[pltpu.VMEM((tm, d), bf16), pltpu.VMEM((tm, d), f32)],
        compiler_params=_params(("parallel", "arbitrary")),
        name="ffn",
    )(x, gpre.reshape(1, d), w_in, w_in, w_out, gpost.reshape(1, d))


def kernel(x, mem, w_in, w_out, w_cq, w_ckv, w_co, w_ffn_in, w_ffn_out, mix_pre_g, mix_post_g, cross_pre_g, cross_post_g, mem_norm_g, ffn_pre_g, ffn_post_g, da_subln_g, hg_onorm_g, lambda_q1, lambda_k1, lambda_q2, lambda_k2, hg_lb_logits, rel_bias):
    batch, seq, d = x.shape
    mem_len = mem.shape[1]
    depth = w_in.shape[0]
    xt = x.reshape(batch * seq, d)
    memt = mem.reshape(batch * mem_len, d)

    for l in range(depth):
        lam_init = 0.8 - 0.6 * math.exp(-0.3 * l)
        proj = _norm_matmul(xt, mix_pre_g[l], w_in[l].astype(bf16), bf16,
                            lead_cols=DA_WIDTH, lead_scale=DA_QK_DIM ** -0.5 * LOG2E)
        o_da = _diff_attention(proj, rel_bias, lambda_q1[l], lambda_k1[l], lambda_q2[l], lambda_k2[l],
                               da_subln_g[l], batch=batch, seq=seq, lam_init=lam_init)
        o_hg = _hgrn2(proj, hg_lb_logits, hg_onorm_g[l], batch=batch, seq=seq, layer=l)
        xt = _mix_out(o_da, o_hg, w_out[l].astype(bf16), mix_post_g[l], xt)
        kv = _norm_matmul(memt, mem_norm_g[l], w_ckv[l].astype(bf16), bf16)
        xt = _cross_attention(xt, kv, w_cq[l].astype(bf16), w_co[l].astype(bf16),
                              cross_pre_g[l], cross_post_g[l], batch=batch, seq=seq, mem_len=mem_len)
        xt = _ffn(xt, w_ffn_in[l].astype(bf16), w_ffn_out[l].astype(bf16), ffn_pre_g[l], ffn_post_g[l])

    return xt.reshape(batch, seq, d)
```

```python
import functools
import math

import jax
import jax.numpy as jnp
from jax import lax
from jax.experimental import pallas as pl
from jax.experimental.pallas import tpu as pltpu

NORM_EPS = 1e-6

DA_HEADS = 8
DA_QK_DIM = 64
DA_V_DIM = 2 * DA_QK_DIM
DA_WIDTH = DA_HEADS * DA_V_DIM
HG_HEADS = 8
HG_DK = 128
HG_DV = 128
HG_WIDTH = HG_HEADS * HG_DV
REL_BUCKETS = 32
REL_MAX_DIST = 128
CX_HEADS = 4
CX_HEAD_DIM = 128
CX_WIDTH = CX_HEADS * CX_HEAD_DIM

LANES = 128
SUBLANES = 8
MASK_VALUE = -1e30
LOG2E = math.log2(math.e)
VMEM_LIMIT = 56 * 1024 * 1024

MM_TM = 1024
MM_TN = 512
OUT_TM = 512
CX_TM = 512
FFN_TM = 512
FFN_TH = 512
DA_TQ = 512
HG_TC = 512
HG_SB = 16

_NT = (((1,), (1,)), ((), ()))
_TN = (((0,), (0,)), ((), ()))

f32 = jnp.float32
bf16 = jnp.bfloat16


def _rms(x, g):
    return x * lax.rsqrt(jnp.mean(x * x, axis=-1, keepdims=True) + NORM_EPS) * g


def _sigmoid(x):
    return 1.0 / (1.0 + jnp.exp(-x))


def _params(sem):
    return pltpu.CompilerParams(dimension_semantics=sem, vmem_limit_bytes=VMEM_LIMIT)


def _norm_matmul_kernel(x_ref, g_ref, w_ref, o_ref, h_ref, *, lead_blocks, lead_scale):
    @pl.when(pl.program_id(1) == 0)
    def _():
        h_ref[...] = _rms(x_ref[...], g_ref[...]).astype(bf16)

    y = jnp.dot(h_ref[...], w_ref[...], preferred_element_type=f32)
    if lead_blocks:
        y = y * jnp.where(pl.program_id(1) < lead_blocks, lead_scale, 1.0)
    o_ref[...] = y.astype(o_ref.dtype)


def _norm_matmul(x, g, w, layer, out_dtype, lead_cols=0, lead_scale=1.0):
    t, d = x.shape
    n = w.shape[2]
    tm, tn = min(MM_TM, t), min(MM_TN, n)
    assert t % tm == 0 and n % tn == 0 and lead_cols % tn == 0
    return pl.pallas_call(
        functools.partial(_norm_matmul_kernel, lead_blocks=lead_cols // tn, lead_scale=lead_scale),
        grid=(t // tm, n // tn),
        in_specs=[
            pl.BlockSpec((tm, d), lambda i, j: (i, 0)),
            pl.BlockSpec((1, d), lambda i, j: (0, 0)),
            pl.BlockSpec((None, d, tn), lambda i, j: (layer, 0, j)),
        ],
        out_specs=pl.BlockSpec((tm, tn), lambda i, j: (i, j)),
        out_shape=jax.ShapeDtypeStruct((t, n), out_dtype),
        scratch_shapes=[pltpu.VMEM((tm, d), bf16)],
        compiler_params=_params(("parallel", "arbitrary")),
        name="norm_matmul",
    )(x, g.reshape(1, d), w)


def _mix_out_kernel(a1_ref, a2_ref, w1_ref, w2_ref, g_ref, x_ref, o_ref):
    y = jnp.dot(a1_ref[...], w1_ref[...], preferred_element_type=f32)
    y = y + jnp.dot(a2_ref[...], w2_ref[...], preferred_element_type=f32)
    o_ref[...] = x_ref[...] + _rms(y, g_ref[...])


def _mix_out(a1, a2, w, layer, g, x):
    t, d = x.shape
    k1, k2 = a1.shape[1], a2.shape[1]
    assert k1 == k2 and w.shape[1] == k1 + k2
    tm = min(OUT_TM, t)
    assert t % tm == 0
    return pl.pallas_call(
        _mix_out_kernel,
        grid=(t // tm,),
        in_specs=[
            pl.BlockSpec((tm, k1), lambda i: (i, 0)),
            pl.BlockSpec((tm, k2), lambda i: (i, 0)),
            pl.BlockSpec((None, k1, d), lambda i: (layer, 0, 0)),
            pl.BlockSpec((None, k2, d), lambda i: (layer, 1, 0)),
            pl.BlockSpec((1, d), lambda i: (0, 0)),
            pl.BlockSpec((tm, d), lambda i: (i, 0)),
        ],
        out_specs=pl.BlockSpec((tm, d), lambda i: (i, 0)),
        out_shape=jax.ShapeDtypeStruct((t, d), f32),
        compiler_params=_params(("parallel",)),
        name="mix_out",
    )(a1, a2, w, w, g.reshape(1, d), x)


def _t5_causal_bucket(dist):
    n = jnp.maximum(dist, 0)
    max_exact = REL_BUCKETS // 2
    nf = jnp.maximum(n, 1).astype(f32)
    large = max_exact + (jnp.log(nf / max_exact) / math.log(REL_MAX_DIST / max_exact)
                         * (REL_BUCKETS - max_exact)).astype(jnp.int32)
    large = jnp.minimum(large, REL_BUCKETS - 1)
    return jnp.where(n < max_exact, n, large)


def _toeplitz(vec, t):
    h = vec.shape[0]
    r = jnp.concatenate([vec[:, ::-1], jnp.zeros((h, 1), vec.dtype)], axis=1)
    g = jnp.tile(r, (1, t))[:, :t * (2 * t - 1)].reshape(h, t, 2 * t - 1)
    return g[:, :, t - 1:]


def _da_bias_tiles(rel_bias, tq):
    d = jnp.arange(-(tq - 1), tq)
    b0 = jnp.where((d >= 0)[:, None], rel_bias[_t5_causal_bucket(d)].astype(f32), MASK_VALUE)
    b1 = rel_bias[_t5_causal_bucket(d + tq)].astype(f32)
    return jnp.stack([_toeplitz(b0.T, tq), _toeplitz(b1.T, tq)], axis=1)


def _da_kernel(q_ref, k_ref, v_ref, bias_ref, far_ref, lq1_ref, lk1_ref, lq2_ref, lk2_ref, g_ref,
               o_ref, m_ref, l_ref, acc_ref, p_ref, alpha_ref, *, tq, lam_init):
    i = pl.program_id(2)
    q = q_ref[...]
    lane = lax.broadcasted_iota(jnp.int32, q.shape, 1)
    zero = jnp.zeros_like(q)
    q_maps = (jnp.where(lane < DA_QK_DIM, q, zero), jnp.where(lane >= DA_QK_DIM, q, zero))

    m_ref[...] = jnp.full(m_ref.shape, MASK_VALUE, f32)
    l_ref[...] = jnp.zeros(l_ref.shape, f32)
    acc_ref[...] = jnp.zeros(acc_ref.shape, f32)

    def rows(j):
        return pl.ds(pl.multiple_of(j * tq, tq), tq)

    def unit(pv_block, pv_slot, qk_block, bias, qk_slot):
        if qk_block is not None:
            kb = k_ref[rows(qk_block), :]
            scores = [lax.dot_general(q_maps[mi], kb, _NT, preferred_element_type=f32) for mi in range(2)]
        if pv_block is not None:
            vb = v_ref[rows(pv_block), :]
            for mi in range(2):
                acc_ref[mi] = alpha_ref[pv_slot, mi] * acc_ref[mi] + jnp.dot(
                    p_ref[pv_slot, mi], vb, preferred_element_type=f32)
        if qk_block is not None:
            for mi in range(2):
                s = scores[mi] if bias is None else scores[mi] + bias
                m_prev = m_ref[mi]
                m_new = jnp.maximum(m_prev, jnp.max(s, axis=-1, keepdims=True))
                alpha = jnp.exp2(m_prev - m_new)
                p = jnp.exp2(s - jnp.tile(m_new, (1, tq // LANES)))
                l_ref[mi] = alpha * l_ref[mi] + jnp.sum(p, axis=-1, keepdims=True)
                m_ref[mi] = m_new
                alpha_ref[qk_slot, mi] = alpha
                p_ref[qk_slot, mi] = p.astype(p_ref.dtype)

    unit(None, None, i, bias_ref[0], 0)

    @pl.when(i == 0)
    def _():
        unit(i, 0, None, None, None)

    @pl.when(i >= 1)
    def _():
        unit(i, 0, i - 1, bias_ref[1], 1)
        m_ref[...] = m_ref[...] - far_ref[...]
        n_far = i - 1

        def before(k):
            return jnp.where(k == 0, i - 1, k - 1)

        def far_pair(t, carry):
            unit(before(2 * t), 1, 2 * t, None, 0)
            unit(2 * t, 0, 2 * t + 1, None, 1)
            return carry

        lax.fori_loop(0, n_far // 2, far_pair, 0)

        @pl.when(n_far % 2 == 1)
        def _():
            unit(before(n_far - 1), 1, n_far - 1, None, 0)
            unit(n_far - 1, 0, None, None, None)

        @pl.when(n_far % 2 == 0)
        def _():
            unit(before(n_far), 1, None, None, None)

    lam = (jnp.exp(jnp.sum(lq1_ref[...] * lk1_ref[...], axis=-1, keepdims=True))
           - jnp.exp(jnp.sum(lq2_ref[...] * lk2_ref[...], axis=-1, keepdims=True)) + lam_init)
    o = acc_ref[0] / l_ref[0] - lam * (acc_ref[1] / l_ref[1])
    o_ref[...] = (_rms(o, g_ref[...]) * (1.0 - lam_init)).astype(o_ref.dtype)


def _diff_attention(proj, rel_bias, lq1, lk1, lq2, lk2, subln_g, *, batch, seq, lam_init):
    tq = min(DA_TQ, seq)
    assert seq % tq == 0 and tq >= REL_MAX_DIST
    nq = seq // tq
    bias = _da_bias_tiles(rel_bias * LOG2E, tq)
    far = jnp.broadcast_to((rel_bias[REL_BUCKETS - 1].astype(f32) * LOG2E)[:, None, None], (DA_HEADS, 1, LANES))
    vec = lambda a: a.reshape(1, DA_QK_DIM).astype(f32)
    lam_spec = pl.BlockSpec((1, DA_QK_DIM), lambda b, h, i: (0, 0))
    return pl.pallas_call(
        functools.partial(_da_kernel, tq=tq, lam_init=lam_init),
        grid=(batch, DA_HEADS, nq),
        in_specs=[
            pl.BlockSpec((tq, DA_V_DIM), lambda b, h, i: (b * nq + i, h)),
            pl.BlockSpec((seq, DA_V_DIM), lambda b, h, i: (b, DA_HEADS + h)),
            pl.BlockSpec((seq, DA_V_DIM), lambda b, h, i: (b, 2 * DA_HEADS + h)),
            pl.BlockSpec((None, 2, tq, tq), lambda b, h, i: (h, 0, 0, 0)),
            pl.BlockSpec((None, 1, LANES), lambda b, h, i: (h, 0, 0)),
            lam_spec, lam_spec, lam_spec, lam_spec,
            pl.BlockSpec((1, DA_V_DIM), lambda b, h, i: (0, 0)),
        ],
        out_specs=pl.BlockSpec((tq, DA_V_DIM), lambda b, h, i: (b * nq + i, h)),
        out_shape=jax.ShapeDtypeStruct((batch * seq, DA_WIDTH), bf16),
        scratch_shapes=[
            pltpu.VMEM((2, tq, LANES), f32),
            pltpu.VMEM((2, tq, LANES), f32),
            pltpu.VMEM((2, tq, DA_V_DIM), f32),
            pltpu.VMEM((2, 2, tq, tq), bf16),
            pltpu.VMEM((2, 2, tq, LANES), f32),
        ],
        compiler_params=_params(("parallel", "parallel", "arbitrary")),
        name="diff_attention",
    )(proj, proj, proj, bias, far, vec(lq1), vec(lk1), vec(lq2), vec(lk2), subln_g.reshape(1, DA_V_DIM))


def _hg_kernel(f_ref, q_ref, i_ref, g_ref, lbl_ref, on_ref, o_ref,
               st_ref, kf_ref, bf_ref, vf_ref, *, layer, tc, sb):
    groups = tc // sb

    @pl.when(pl.program_id(2) == 0)
    def _():
        st_ref[...] = jnp.zeros(st_ref.shape, f32)

    lg = lbl_ref[...]
    pe = jnp.exp(lg - jnp.max(lg, axis=0, keepdims=True))
    lb = jnp.zeros((1, lg.shape[1]), f32)
    for r in range(1, layer + 1):
        lb = lb + pe[r:r + 1]
    lb = lb / jnp.sum(pe, axis=0, keepdims=True)

    f = lb + (1.0 - lb) * _sigmoid(f_ref[...].astype(f32))
    k = 1.0 - f
    b = jnp.log2(f)
    r = lax.broadcasted_iota(jnp.int32, b.shape, 0) % sb
    sh = 1
    while sh < sb:
        b = b + jnp.where(r >= sh, pltpu.roll(b, sh, axis=0), 0.0)
        sh *= 2
    q = q_ref[...].astype(f32)
    v = i_ref[...].astype(f32)
    kf_ref[...] = k
    bf_ref[...] = b
    vf_ref[...] = v

    def source_row(ref, s):
        return jnp.stack([jnp.broadcast_to(ref[sb * g + s:sb * g + s + 1, :], (SUBLANES, ref.shape[1]))
                          for g in range(groups)])

    def pieces(x):
        lo = jnp.stack([x[sb * g:sb * g + SUBLANES] for g in range(groups)])
        hi = jnp.stack([x[sb * g + SUBLANES:sb * (g + 1)] for g in range(groups)])
        return lo, hi

    q_lo, q_hi = pieces(q)
    k_lo, k_hi = pieces(k)
    b_lo, b_hi = pieces(b)
    v_lo, v_hi = pieces(v)
    b_end = b_hi[:, SUBLANES - 1:SUBLANES, :]

    qd_lo, qd_hi = q_lo * jnp.exp2(b_lo), q_hi * jnp.exp2(b_hi)
    kd_lo, kd_hi = k_lo * jnp.exp2(b_end - b_lo), k_hi * jnp.exp2(b_end - b_hi)
    block_decay = jnp.exp2(b_end[:, 0, :])
    vb = v.astype(bf16)
    increments = []
    for g in range(groups):
        kd = jnp.concatenate([kd_lo[g], kd_hi[g]], axis=0).astype(bf16)
        increments.append(lax.dot_general(vb[sb * g:sb * (g + 1)], kd, _TN, preferred_element_type=f32))
    st = st_ref[...]
    states = []
    for g in range(groups):
        states.append(st.astype(bf16))
        st = block_decay[g:g + 1] * st + increments[g]
    st_ref[...] = st
    o_state = []
    for g in range(groups):
        qd = jnp.concatenate([qd_lo[g], qd_hi[g]], axis=0).astype(bf16)
        o_state.append(lax.dot_general(qd, states[g], _NT, preferred_element_type=f32))

    trow = lax.broadcasted_iota(jnp.int32, (1, SUBLANES, 1), 1)

    def contrib(q_p, b_p, ks, bs, vs, diag_row):
        d = b_p - bs
        if diag_row is not None:
            d = jnp.minimum(d, 0.0)
        a = jnp.sum(q_p * ks * jnp.exp2(d), axis=-1, keepdims=True)
        if diag_row is not None:
            a = jnp.where(trow >= diag_row, a, 0.0)
        return a * vs

    o_lo = jnp.zeros(q_lo.shape, f32)
    o_hi = jnp.zeros(q_hi.shape, f32)
    for s in range(SUBLANES):
        ks, bs, vs = source_row(kf_ref, s), source_row(bf_ref, s), source_row(vf_ref, s)
        o_lo = o_lo + contrib(q_lo, b_lo, ks, bs, vs, s)
        o_hi = o_hi + contrib(q_hi, b_hi, ks, bs, vs, None)
    for s in range(SUBLANES):
        ks, bs, vs = (source_row(kf_ref, SUBLANES + s), source_row(bf_ref, SUBLANES + s),
                      source_row(vf_ref, SUBLANES + s))
        o_hi = o_hi + contrib(q_hi, b_hi, ks, bs, vs, s)
    o = jnp.concatenate(
        [jnp.concatenate([o_lo[g], o_hi[g]], axis=0) + o_state[g] for g in range(groups)], axis=0)

    gate = g_ref[...].astype(f32)
    o_ref[...] = (_rms(o, on_ref[...]) * (gate * _sigmoid(gate))).astype(o_ref.dtype)


def _hgrn2(proj, lb_logits, onorm_g, *, batch, seq, layer):
    tc = min(HG_TC, seq)
    assert seq % tc == 0 and tc % HG_SB == 0 and HG_SB == 2 * SUBLANES
    nc = seq // tc
    depth = lb_logits.shape[0]
    first = (proj.shape[1] - 4 * HG_WIDTH) // HG_DK
    col = lambda part: pl.BlockSpec((tc, HG_DK), lambda b, h, c: (b * nc + c, first + part * HG_HEADS + h))
    return pl.pallas_call(
        functools.partial(_hg_kernel, layer=layer, tc=tc, sb=HG_SB),
        grid=(batch, HG_HEADS, nc),
        in_specs=[
            col(0), col(1), col(2), col(3),
            pl.BlockSpec((depth, HG_DK), lambda b, h, c: (0, h)),
            pl.BlockSpec((1, HG_DV), lambda b, h, c: (0, 0)),
        ],
        out_specs=pl.BlockSpec((tc, HG_DV), lambda b, h, c: (b * nc + c, h)),
        out_shape=jax.ShapeDtypeStruct((batch * seq, HG_WIDTH), bf16),
        scratch_shapes=[pltpu.VMEM((HG_DV, HG_DK), f32)] + [pltpu.VMEM((tc, HG_DK), f32)] * 3,
        compiler_params=_params(("parallel", "parallel", "arbitrary")),
        name="hgrn2",
    )(proj, proj, proj, proj, lb_logits.astype(f32), onorm_g.reshape(1, HG_DV))


def _cross_kernel(x_ref, gpre_ref, wq_ref, kv_ref, wo_ref, gpost_ref, o_ref):
    x = x_ref[...]
    h = _rms(x, gpre_ref[...]).astype(bf16)
    q = jnp.dot(h, wq_ref[...], preferred_element_type=f32).astype(bf16)
    outs = []
    for hd in range(CX_HEADS):
        cols = slice(hd * CX_HEAD_DIM, (hd + 1) * CX_HEAD_DIM)
        kh = kv_ref[:, cols]
        vh = kv_ref[:, CX_WIDTH + hd * CX_HEAD_DIM:CX_WIDTH + (hd + 1) * CX_HEAD_DIM]
        s = lax.dot_general(q[:, cols], kh, _NT, preferred_element_type=f32) * CX_HEAD_DIM ** -0.5
        p = jnp.exp(s - jnp.max(s, axis=-1, keepdims=True))
        oh = jnp.dot(p.astype(bf16), vh, preferred_element_type=f32)
        outs.append((oh / jnp.sum(p, axis=-1, keepdims=True)).astype(bf16))
    o = jnp.concatenate(outs, axis=-1)
    y = jnp.dot(o, wo_ref[...], preferred_element_type=f32)
    o_ref[...] = x + _rms(y, gpost_ref[...])


def _cross_attention(x, kv, wq, wo, layer, gpre, gpost, *, batch, seq, mem_len):
    t, d = x.shape
    tm = min(CX_TM, seq)
    assert seq % tm == 0
    per_batch = seq // tm
    return pl.pallas_call(
        _cross_kernel,
        grid=(t // tm,),
        in_specs=[
            pl.BlockSpec((tm, d), lambda i: (i, 0)),
            pl.BlockSpec((1, d), lambda i: (0, 0)),
            pl.BlockSpec((None, d, CX_WIDTH), lambda i: (layer, 0, 0)),
            pl.BlockSpec((mem_len, 2 * CX_WIDTH), lambda i: (i // per_batch, 0)),
            pl.BlockSpec((None, CX_WIDTH, d), lambda i: (layer, 0, 0)),
            pl.BlockSpec((1, d), lambda i: (0, 0)),
        ],
        out_specs=pl.BlockSpec((tm, d), lambda i: (i, 0)),
        out_shape=jax.ShapeDtypeStruct((t, d), f32),
        compiler_params=_params(("parallel",)),
        name="cross_attention",
    )(x, gpre.reshape(1, d), wq, kv, wo, gpost.reshape(1, d))


def _ffn_kernel(x_ref, gpre_ref, wg_ref, wu_ref, wo_ref, gpost_ref, o_ref, h_ref, acc_ref):
    j = pl.program_id(1)

    @pl.when(j == 0)
    def _():
        h_ref[...] = _rms(x_ref[...], gpre_ref[...]).astype(bf16)
        acc_ref[...] = jnp.zeros(acc_ref.shape, f32)

    h = h_ref[...]
    gate = jnp.dot(h, wg_ref[...], preferred_element_type=f32)
    up = jnp.dot(h, wu_ref[...], preferred_element_type=f32)
    a = (gate * _sigmoid(gate) * up).astype(bf16)
    acc_ref[...] += jnp.dot(a, wo_ref[...], preferred_element_type=f32)

    @pl.when(j == pl.num_programs(1) - 1)
    def _():
        o_ref[...] = x_ref[...] + _rms(acc_ref[...], gpost_ref[...])


def _ffn(x, w_in, w_out, layer, gpre, gpost):
    t, d = x.shape
    hidden = w_out.shape[1]
    tm, th = min(FFN_TM, t), min(FFN_TH, hidden)
    assert t % tm == 0 and hidden % th == 0 and w_in.shape[2] == 2 * hidden
    nh = hidden // th
    return pl.pallas_call(
        _ffn_kernel,
        grid=(t // tm, nh),
        in_specs=[
            pl.BlockSpec((tm, d), lambda i, j: (i, 0)),
            pl.BlockSpec((1, d), lambda i, j: (0, 0)),
            pl.BlockSpec((None, d, th), lambda i, j: (layer, 0, j)),
            pl.BlockSpec((None, d, th), lambda i, j: (layer, 0, nh + j)),
            pl.BlockSpec((None, th, d), lambda i, j: (layer, j, 0)),
            pl.BlockSpec((1, d), lambda i, j: (0, 0)),
        ],
        out_specs=pl.BlockSpec((tm, d), lambda i, j: (i, 0)),
        out_shape=jax.ShapeDtypeStruct((t, d), f32),
        scratch_shapes=[pltpu.VMEM((tm, d), bf16), pltpu.VMEM((tm, d), f32)],
        compiler_params=_params(("parallel", "arbitrary")),
        name="ffn",
    )(x, gpre.reshape(1, d), w_in, w_in, w_out, gpost.reshape(1, d))


def kernel(x, mem, w_in, w_out, w_cq, w_ckv, w_co, w_ffn_in, w_ffn_out, mix_pre_g, mix_post_g, cross_pre_g, cross_post_g, mem_norm_g, ffn_pre_g, ffn_post_g, da_subln_g, hg_onorm_g, lambda_q1, lambda_k1, lambda_q2, lambda_k2, hg_lb_logits, rel_bias):
    batch, seq, d = x.shape
    mem_len = mem.shape[1]
    depth = w_in.shape[0]
    xt = x.reshape(batch * seq, d)
    memt = mem.reshape(batch * mem_len, d)

    w_in, w_out, w_cq, w_ckv, w_co, w_ffn_in, w_ffn_out = (
        w.astype(bf16) for w in (w_in, w_out, w_cq, w_ckv, w_co, w_ffn_in, w_ffn_out))

    for l in range(depth):
        lam_init = 0.8 - 0.6 * math.exp(-0.3 * l)
        proj = _norm_matmul(xt, mix_pre_g[l], w_in, l, bf16,
                            lead_cols=DA_WIDTH, lead_scale=DA_QK_DIM ** -0.5 * LOG2E)
        o_da = _diff_attention(proj, rel_bias, lambda_q1[l], lambda_k1[l], lambda_q2[l], lambda_k2[l],
                               da_subln_g[l], batch=batch, seq=seq, lam_init=lam_init)
        o_hg = _hgrn2(proj, hg_lb_logits, hg_onorm_g[l], batch=batch, seq=seq, layer=l)
        xt = _mix_out(o_da, o_hg, w_out, l, mix_post_g[l], xt)
        kv = _norm_matmul(memt, mem_norm_g[l], w_ckv, l, bf16)
        xt = _cross_attention(xt, kv, w_cq, w_co, l, cross_pre_g[l], cross_post_g[l],
                              batch=batch, seq=seq, mem_len=mem_len)
        xt = _ffn(xt, w_ffn_in, w_ffn_out, l, ffn_pre_g[l], ffn_post_g[l])

    return xt.reshape(batch, seq, d)
```

```python
import functools
import math

import jax
import jax.numpy as jnp
from jax import lax
from jax.experimental import pallas as pl
from jax.experimental.pallas import tpu as pltpu

NORM_EPS = 1e-6

DA_HEADS = 8
DA_QK_DIM = 64
DA_V_DIM = 2 * DA_QK_DIM
DA_WIDTH = DA_HEADS * DA_V_DIM
HG_HEADS = 8
HG_DK = 128
HG_DV = 128
HG_WIDTH = HG_HEADS * HG_DV
REL_BUCKETS = 32
REL_MAX_DIST = 128
CX_HEADS = 4
CX_HEAD_DIM = 128
CX_WIDTH = CX_HEADS * CX_HEAD_DIM

LANES = 128
SUBLANES = 8
MASK_VALUE = -1e30
LOG2E = math.log2(math.e)
VMEM_LIMIT = 56 * 1024 * 1024

MM_TM = 1024
MM_TN = 1024
OUT_TM = 512
CX_TM = 512
FFN_TM = 512
FFN_TH = 512
MIX_TILE = 512
HG_SB = 16
HG_BATCH = 4

_NT = (((1,), (1,)), ((), ()))
_TN = (((0,), (0,)), ((), ()))

f32 = jnp.float32
bf16 = jnp.bfloat16


def _rms(x, g):
    return x * lax.rsqrt(jnp.mean(x * x, axis=-1, keepdims=True) + NORM_EPS) * g


def _sigmoid(x):
    return 1.0 / (1.0 + jnp.exp(-x))


def _params(sem):
    return pltpu.CompilerParams(dimension_semantics=sem, vmem_limit_bytes=VMEM_LIMIT)


def _norm_matmul_kernel(x_ref, g_ref, w_ref, o_ref, h_ref, *, lead_blocks, lead_scale):
    @pl.when(pl.program_id(1) == 0)
    def _():
        h_ref[...] = _rms(x_ref[...], g_ref[...]).astype(bf16)

    y = jnp.dot(h_ref[...], w_ref[...], preferred_element_type=f32)
    if lead_blocks:
        y = y * jnp.where(pl.program_id(1) < lead_blocks, lead_scale, 1.0)
    o_ref[...] = y.astype(o_ref.dtype)


def _norm_matmul(x, g, w, layer, out_dtype, lead_cols=0, lead_scale=1.0):
    t, d = x.shape
    n = w.shape[2]
    tm, tn = min(MM_TM, t), min(MM_TN, n)
    assert t % tm == 0 and n % tn == 0 and lead_cols % tn == 0
    return pl.pallas_call(
        functools.partial(_norm_matmul_kernel, lead_blocks=lead_cols // tn, lead_scale=lead_scale),
        grid=(t // tm, n // tn),
        in_specs=[
            pl.BlockSpec((tm, d), lambda i, j: (i, 0)),
            pl.BlockSpec((1, d), lambda i, j: (0, 0)),
            pl.BlockSpec((None, d, tn), lambda i, j: (layer, 0, j)),
        ],
        out_specs=pl.BlockSpec((tm, tn), lambda i, j: (i, j)),
        out_shape=jax.ShapeDtypeStruct((t, n), out_dtype),
        scratch_shapes=[pltpu.VMEM((tm, d), bf16)],
        compiler_params=_params(("parallel", "arbitrary")),
        name="norm_matmul",
    )(x, g.reshape(1, d), w)


def _mix_out_kernel(a1_ref, a2_ref, w1_ref, w2_ref, g_ref, x_ref, o_ref):
    y = jnp.dot(a1_ref[...], w1_ref[...], preferred_element_type=f32)
    y = y + jnp.dot(a2_ref[...], w2_ref[...], preferred_element_type=f32)
    o_ref[...] = x_ref[...] + _rms(y, g_ref[...])


def _mix_out(a1, a2, w, layer, g, x):
    t, d = x.shape
    k1, k2 = a1.shape[1], a2.shape[1]
    assert k1 == k2 and w.shape[1] == k1 + k2
    tm = min(OUT_TM, t)
    assert t % tm == 0
    return pl.pallas_call(
        _mix_out_kernel,
        grid=(t // tm,),
        in_specs=[
            pl.BlockSpec((tm, k1), lambda i: (i, 0)),
            pl.BlockSpec((tm, k2), lambda i: (i, 0)),
            pl.BlockSpec((None, k1, d), lambda i: (layer, 0, 0)),
            pl.BlockSpec((None, k2, d), lambda i: (layer, 1, 0)),
            pl.BlockSpec((1, d), lambda i: (0, 0)),
            pl.BlockSpec((tm, d), lambda i: (i, 0)),
        ],
        out_specs=pl.BlockSpec((tm, d), lambda i: (i, 0)),
        out_shape=jax.ShapeDtypeStruct((t, d), f32),
        compiler_params=_params(("parallel",)),
        name="mix_out",
    )(a1, a2, w, w, g.reshape(1, d), x)


def _t5_causal_bucket(dist):
    n = jnp.maximum(dist, 0)
    max_exact = REL_BUCKETS // 2
    nf = jnp.maximum(n, 1).astype(f32)
    large = max_exact + (jnp.log(nf / max_exact) / math.log(REL_MAX_DIST / max_exact)
                         * (REL_BUCKETS - max_exact)).astype(jnp.int32)
    large = jnp.minimum(large, REL_BUCKETS - 1)
    return jnp.where(n < max_exact, n, large)


def _toeplitz(vec, t):
    h = vec.shape[0]
    r = jnp.concatenate([vec[:, ::-1], jnp.zeros((h, 1), vec.dtype)], axis=1)
    g = jnp.tile(r, (1, t))[:, :t * (2 * t - 1)].reshape(h, t, 2 * t - 1)
    return g[:, :, t - 1:]


def _da_bias_tiles(rel_bias, tq):
    d = jnp.arange(-(tq - 1), tq)
    b0 = jnp.where((d >= 0)[:, None], rel_bias[_t5_causal_bucket(d)].astype(f32), MASK_VALUE)
    b1 = rel_bias[_t5_causal_bucket(d + tq)].astype(f32)
    return jnp.stack([_toeplitz(b0.T, tq), _toeplitz(b1.T, tq)], axis=1)


def _hg_step(f_ref, q_ref, i_ref, g_ref, lbl_ref, on_ref, o_ref, st_ref, kf_ref, bf_ref, vf_ref,
             *, layer, tc, sb):
    groups = tc // sb

    lg = lbl_ref[...]
    pe = jnp.exp(lg - jnp.max(lg, axis=0, keepdims=True))
    lb = jnp.zeros((1, lg.shape[1]), f32)
    for r in range(1, layer + 1):
        lb = lb + pe[r:r + 1]
    lb = lb / jnp.sum(pe, axis=0, keepdims=True)

    f = lb + (1.0 - lb) * _sigmoid(f_ref[...].astype(f32))
    k = 1.0 - f
    b = jnp.log2(f)
    r = lax.broadcasted_iota(jnp.int32, b.shape, 0) % sb
    sh = 1
    while sh < sb:
        b = b + jnp.where(r >= sh, pltpu.roll(b, sh, axis=0), 0.0)
        sh *= 2
    q = q_ref[...].astype(f32)
    v = i_ref[...].astype(f32)
    kf_ref[...] = k
    bf_ref[...] = b
    vf_ref[...] = v

    def source_row(ref, s, g0):
        return jnp.stack([jnp.broadcast_to(ref[sb * g + s:sb * g + s + 1, :], (SUBLANES, ref.shape[1]))
                          for g in range(g0, g0 + HG_BATCH)])

    def pieces(x):
        lo = jnp.stack([x[sb * g:sb * g + SUBLANES] for g in range(groups)])
        hi = jnp.stack([x[sb * g + SUBLANES:sb * (g + 1)] for g in range(groups)])
        return lo, hi

    q_lo, q_hi = pieces(q)
    k_lo, k_hi = pieces(k)
    b_lo, b_hi = pieces(b)
    b_end = b_hi[:, SUBLANES - 1:SUBLANES, :]

    qd_lo, qd_hi = q_lo * jnp.exp2(b_lo), q_hi * jnp.exp2(b_hi)
    kd_lo, kd_hi = k_lo * jnp.exp2(b_end - b_lo), k_hi * jnp.exp2(b_end - b_hi)
    block_decay = jnp.exp2(b_end[:, 0, :])
    vb = v.astype(bf16)
    increments = []
    for g in range(groups):
        kd = jnp.concatenate([kd_lo[g], kd_hi[g]], axis=0).astype(bf16)
        increments.append(lax.dot_general(vb[sb * g:sb * (g + 1)], kd, _TN, preferred_element_type=f32))
    st = st_ref[...]
    states = []
    for g in range(groups):
        states.append(st.astype(bf16))
        st = block_decay[g:g + 1] * st + increments[g]
    st_ref[...] = st
    o_state = []
    for g in range(groups):
        qd = jnp.concatenate([qd_lo[g], qd_hi[g]], axis=0).astype(bf16)
        o_state.append(lax.dot_general(qd, states[g], _NT, preferred_element_type=f32))

    trow = lax.broadcasted_iota(jnp.int32, (1, SUBLANES, 1), 1)

    def contrib(q_p, b_p, ks, bs, vs, diag_row):
        d = b_p - bs
        if diag_row is not None:
            d = jnp.minimum(d, 0.0)
        a = jnp.sum(q_p * ks * jnp.exp2(d), axis=-1, keepdims=True)
        if diag_row is not None:
            a = jnp.where(trow >= diag_row, a, 0.0)
        return a * vs

    o_blocks = []
    for g0 in range(0, groups, HG_BATCH):
        batch = slice(g0, g0 + HG_BATCH)
        ql, qh, bl, bh = q_lo[batch], q_hi[batch], b_lo[batch], b_hi[batch]
        o_lo = jnp.zeros(ql.shape, f32)
        o_hi = jnp.zeros(qh.shape, f32)
        for s in range(SUBLANES):
            ks, bs, vs = source_row(kf_ref, s, g0), source_row(bf_ref, s, g0), source_row(vf_ref, s, g0)
            o_lo = o_lo + contrib(ql, bl, ks, bs, vs, s)
            o_hi = o_hi + contrib(qh, bh, ks, bs, vs, None)
        for s in range(SUBLANES, sb):
            ks, bs, vs = source_row(kf_ref, s, g0), source_row(bf_ref, s, g0), source_row(vf_ref, s, g0)
            o_hi = o_hi + contrib(qh, bh, ks, bs, vs, s - SUBLANES)
        o_blocks += [jnp.concatenate([o_lo[g], o_hi[g]], axis=0) + o_state[g0 + g] for g in range(HG_BATCH)]
    o = jnp.concatenate(o_blocks, axis=0)

    gate = g_ref[...].astype(f32)
    o_ref[...] = (_rms(o, on_ref[...]) * (gate * _sigmoid(gate))).astype(o_ref.dtype)


def _mixer_kernel(q_ref, k_ref, v_ref, bias_ref, far_ref, lq1_ref, lk1_ref, lq2_ref, lk2_ref, sg_ref,
                  hf_ref, hq_ref, hi_ref, hg_ref, lbl_ref, on_ref,
                  oda_ref, ohg_ref,
                  m_ref, l_ref, acc_ref, p_ref, alpha_ref, st_ref, kf_ref, bf_ref, vf_ref,
                  *, tq, lam_init, layer, sb):
    i = pl.program_id(2)

    @pl.when(i == 0)
    def _():
        st_ref[...] = jnp.zeros(st_ref.shape, f32)

    q = q_ref[...]
    lane = lax.broadcasted_iota(jnp.int32, q.shape, 1)
    zero = jnp.zeros_like(q)
    q_maps = (jnp.where(lane < DA_QK_DIM, q, zero), jnp.where(lane >= DA_QK_DIM, q, zero))

    m_ref[...] = jnp.full(m_ref.shape, MASK_VALUE, f32)
    l_ref[...] = jnp.zeros(l_ref.shape, f32)
    acc_ref[...] = jnp.zeros(acc_ref.shape, f32)

    def rows(j):
        return pl.ds(pl.multiple_of(j * tq, tq), tq)

    def unit(pv_block, pv_slot, qk_block, bias, qk_slot):
        if qk_block is not None:
            kb = k_ref[rows(qk_block), :]
            scores = [lax.dot_general(q_maps[mi], kb, _NT, preferred_element_type=f32) for mi in range(2)]
        if pv_block is not None:
            vb = v_ref[rows(pv_block), :]
            for mi in range(2):
                acc_ref[mi] = alpha_ref[pv_slot, mi] * acc_ref[mi] + jnp.dot(
                    p_ref[pv_slot, mi], vb, preferred_element_type=f32)
        if qk_block is not None:
            for mi in range(2):
                s = scores[mi] if bias is None else scores[mi] + bias
                m_prev = m_ref[mi]
                m_new = jnp.maximum(m_prev, jnp.max(s, axis=-1, keepdims=True))
                alpha = jnp.exp2(m_prev - m_new)
                p = jnp.exp2(s - jnp.tile(m_new, (1, tq // LANES)))
                l_ref[mi] = alpha * l_ref[mi] + jnp.sum(p, axis=-1, keepdims=True)
                m_ref[mi] = m_new
                alpha_ref[qk_slot, mi] = alpha
                p_ref[qk_slot, mi] = p.astype(p_ref.dtype)

    unit(None, None, i, bias_ref[0], 0)

    _hg_step(hf_ref, hq_ref, hi_ref, hg_ref, lbl_ref, on_ref, ohg_ref, st_ref, kf_ref, bf_ref, vf_ref,
             layer=layer, tc=tq, sb=sb)

    @pl.when(i == 0)
    def _():
        unit(i, 0, None, None, None)

    @pl.when(i >= 1)
    def _():
        unit(i, 0, i - 1, bias_ref[1], 1)
        m_ref[...] = m_ref[...] - far_ref[...]
        n_far = i - 1

        def before(k):
            return jnp.where(k == 0, i - 1, k - 1)

        def far_pair(t, carry):
            unit(before(2 * t), 1, 2 * t, None, 0)
            unit(2 * t, 0, 2 * t + 1, None, 1)
            return carry

        lax.fori_loop(0, n_far // 2, far_pair, 0)

        @pl.when(n_far % 2 == 1)
        def _():
            unit(before(n_far - 1), 1, n_far - 1, None, 0)
            unit(n_far - 1, 0, None, None, None)

        @pl.when(n_far % 2 == 0)
        def _():
            unit(before(n_far), 1, None, None, None)

    lam = (jnp.exp(jnp.sum(lq1_ref[...] * lk1_ref[...], axis=-1, keepdims=True))
           - jnp.exp(jnp.sum(lq2_ref[...] * lk2_ref[...], axis=-1, keepdims=True)) + lam_init)
    o = acc_ref[0] / l_ref[0] - lam * (acc_ref[1] / l_ref[1])
    oda_ref[...] = (_rms(o, sg_ref[...]) * (1.0 - lam_init)).astype(oda_ref.dtype)


def _token_mixer(proj, rel_bias, lq1, lk1, lq2, lk2, subln_g, lb_logits, onorm_g, *, batch, seq, layer, lam_init):
    tq = min(MIX_TILE, seq)
    assert seq % tq == 0 and tq >= REL_MAX_DIST and tq % HG_SB == 0 and HG_SB == 2 * SUBLANES
    assert DA_HEADS == HG_HEADS and DA_V_DIM == HG_DK == HG_DV == LANES
    nq = seq // tq
    depth = lb_logits.shape[0]
    bias = _da_bias_tiles(rel_bias * LOG2E, tq)
    far = jnp.broadcast_to((rel_bias[REL_BUCKETS - 1].astype(f32) * LOG2E)[:, None, None], (DA_HEADS, 1, LANES))
    vec = lambda a: a.reshape(1, DA_QK_DIM).astype(f32)
    lam_spec = pl.BlockSpec((1, DA_QK_DIM), lambda b, h, i: (0, 0))
    tile = lambda group: pl.BlockSpec((tq, LANES), lambda b, h, i: (b * nq + i, group * DA_HEADS + h))
    whole = lambda group: pl.BlockSpec((seq, LANES), lambda b, h, i: (b, group * DA_HEADS + h))
    gain = pl.BlockSpec((1, LANES), lambda b, h, i: (0, 0))
    out = pl.BlockSpec((tq, LANES), lambda b, h, i: (b * nq + i, h))
    return pl.pallas_call(
        functools.partial(_mixer_kernel, tq=tq, lam_init=lam_init, layer=layer, sb=HG_SB),
        grid=(batch, DA_HEADS, nq),
        in_specs=[
            tile(0), whole(1), whole(2),
            pl.BlockSpec((None, 2, tq, tq), lambda b, h, i: (h, 0, 0, 0)),
            pl.BlockSpec((None, 1, LANES), lambda b, h, i: (h, 0, 0)),
            lam_spec, lam_spec, lam_spec, lam_spec, gain,
            tile(3), tile(4), tile(5), tile(6),
            pl.BlockSpec((depth, LANES), lambda b, h, i: (0, h)),
            gain,
        ],
        out_specs=[out, out],
        out_shape=[jax.ShapeDtypeStruct((batch * seq, DA_WIDTH), bf16),
                   jax.ShapeDtypeStruct((batch * seq, HG_WIDTH), bf16)],
        scratch_shapes=[
            pltpu.VMEM((2, tq, LANES), f32),
            pltpu.VMEM((2, tq, LANES), f32),
            pltpu.VMEM((2, tq, DA_V_DIM), f32),
            pltpu.VMEM((2, 2, tq, tq), bf16),
            pltpu.VMEM((2, 2, tq, LANES), f32),
            pltpu.VMEM((HG_DV, HG_DK), f32),
            pltpu.VMEM((tq, HG_DK), f32),
            pltpu.VMEM((tq, HG_DK), f32),
            pltpu.VMEM((tq, HG_DV), f32),
        ],
        compiler_params=_params(("parallel", "parallel", "arbitrary")),
        name="token_mixer",
    )(proj, proj, proj, bias, far, vec(lq1), vec(lk1), vec(lq2), vec(lk2), subln_g.reshape(1, DA_V_DIM),
      proj, proj, proj, proj, lb_logits.astype(f32), onorm_g.reshape(1, HG_DV))


def _cross_kernel(x_ref, gpre_ref, wq_ref, kv_ref, wo_ref, gpost_ref, o_ref):
    x = x_ref[...]
    h = _rms(x, gpre_ref[...]).astype(bf16)
    q = jnp.dot(h, wq_ref[...], preferred_element_type=f32).astype(bf16)
    outs = []
    for hd in range(CX_HEADS):
        cols = slice(hd * CX_HEAD_DIM, (hd + 1) * CX_HEAD_DIM)
        kh = kv_ref[:, cols]
        vh = kv_ref[:, CX_WIDTH + hd * CX_HEAD_DIM:CX_WIDTH + (hd + 1) * CX_HEAD_DIM]
        s = lax.dot_general(q[:, cols], kh, _NT, preferred_element_type=f32) * CX_HEAD_DIM ** -0.5
        p = jnp.exp(s - jnp.max(s, axis=-1, keepdims=True))
        oh = jnp.dot(p.astype(bf16), vh, preferred_element_type=f32)
        outs.append((oh / jnp.sum(p, axis=-1, keepdims=True)).astype(bf16))
    o = jnp.concatenate(outs, axis=-1)
    y = jnp.dot(o, wo_ref[...], preferred_element_type=f32)
    o_ref[...] = x + _rms(y, gpost_ref[...])


def _cross_attention(x, kv, wq, wo, layer, gpre, gpost, *, batch, seq, mem_len):
    t, d = x.shape
    tm = min(CX_TM, seq)
    assert seq % tm == 0
    per_batch = seq // tm
    return pl.pallas_call(
        _cross_kernel,
        grid=(t // tm,),
        in_specs=[
            pl.BlockSpec((tm, d), lambda i: (i, 0)),
            pl.BlockSpec((1, d), lambda i: (0, 0)),
            pl.BlockSpec((None, d, CX_WIDTH), lambda i: (layer, 0, 0)),
            pl.BlockSpec((mem_len, 2 * CX_WIDTH), lambda i: (i // per_batch, 0)),
            pl.BlockSpec((None, CX_WIDTH, d), lambda i: (layer, 0, 0)),
            pl.BlockSpec((1, d), lambda i: (0, 0)),
        ],
        out_specs=pl.BlockSpec((tm, d), lambda i: (i, 0)),
        out_shape=jax.ShapeDtypeStruct((t, d), f32),
        compiler_params=_params(("parallel",)),
        name="cross_attention",
    )(x, gpre.reshape(1, d), wq, kv, wo, gpost.reshape(1, d))


def _ffn_kernel(x_ref, gpre_ref, wg_ref, wu_ref, wo_ref, gpost_ref, o_ref, h_ref, acc_ref):
    j = pl.program_id(1)

    @pl.when(j == 0)
    def _():
        h_ref[...] = _rms(x_ref[...], gpre_ref[...]).astype(bf16)
        acc_ref[...] = jnp.zeros(acc_ref.shape, f32)

    h = h_ref[...]
    gate = jnp.dot(h, wg_ref[...], preferred_element_type=f32)
    up = jnp.dot(h, wu_ref[...], preferred_element_type=f32)
    a = (gate * _sigmoid(gate) * up).astype(bf16)
    acc_ref[...] += jnp.dot(a, wo_ref[...], preferred_element_type=f32)

    @pl.when(j == pl.num_programs(1) - 1)
    def _():
        o_ref[...] = x_ref[...] + _rms(acc_ref[...], gpost_ref[...])


def _ffn(x, w_in, w_out, layer, gpre, gpost):
    t, d = x.shape
    hidden = w_out.shape[1]
    tm, th = min(FFN_TM, t), min(FFN_TH, hidden)
    assert t % tm == 0 and hidden % th == 0 and w_in.shape[2] == 2 * hidden
    nh = hidden // th
    return pl.pallas_call(
        _ffn_kernel,
        grid=(t // tm, nh),
        in_specs=[
            pl.BlockSpec((tm, d), lambda i, j: (i, 0)),
            pl.BlockSpec((1, d), lambda i, j: (0, 0)),
            pl.BlockSpec((None, d, th), lambda i, j: (layer, 0, j)),
            pl.BlockSpec((None, d, th), lambda i, j: (layer, 0, nh + j)),
            pl.BlockSpec((None, th, d), lambda i, j: (layer, j, 0)),
            pl.BlockSpec((1, d), lambda i, j: (0, 0)),
        ],
        out_specs=pl.BlockSpec((tm, d), lambda i, j: (i, 0)),
        out_shape=jax.ShapeDtypeStruct((t, d), f32),
        scratch_shapes=[pltpu.VMEM((tm, d), bf16), pltpu.VMEM((tm, d), f32)],
        compiler_params=_params(("parallel", "arbitrary")),
        name="ffn",
    )(x, gpre.reshape(1, d), w_in, w_in, w_out, gpost.reshape(1, d))


def kernel(x, mem, w_in, w_out, w_cq, w_ckv, w_co, w_ffn_in, w_ffn_out, mix_pre_g, mix_post_g, cross_pre_g, cross_post_g, mem_norm_g, ffn_pre_g, ffn_post_g, da_subln_g, hg_onorm_g, lambda_q1, lambda_k1, lambda_q2, lambda_k2, hg_lb_logits, rel_bias):
    batch, seq, d = x.shape
    mem_len = mem.shape[1]
    depth = w_in.shape[0]
    xt = x.reshape(batch * seq, d)
    memt = mem.reshape(batch * mem_len, d)
    w_in, w_out, w_cq, w_ckv, w_co, w_ffn_in, w_ffn_out = (
        w.astype(bf16) for w in (w_in, w_out, w_cq, w_ckv, w_co, w_ffn_in, w_ffn_out))

    for l in range(depth):
        lam_init = 0.8 - 0.6 * math.exp(-0.3 * l)
        proj = _norm_matmul(xt, mix_pre_g[l], w_in, l, bf16,
                            lead_cols=DA_WIDTH, lead_scale=DA_QK_DIM ** -0.5 * LOG2E)
        o_da, o_hg = _token_mixer(proj, rel_bias, lambda_q1[l], lambda_k1[l], lambda_q2[l], lambda_k2[l],
                                  da_subln_g[l], hg_lb_logits, hg_onorm_g[l],
                                  batch=batch, seq=seq, layer=l, lam_init=lam_init)
        xt = _mix_out(o_da, o_hg, w_out, l, mix_post_g[l], xt)
        kv = _norm_matmul(memt, mem_norm_g[l], w_ckv, l, bf16)
        xt = _cross_attention(xt, kv, w_cq, w_co, l, cross_pre_g[l], cross_post_g[l],
                              batch=batch, seq=seq, mem_len=mem_len)
        xt = _ffn(xt, w_ffn_in, w_ffn_out, l, ffn_pre_g[l], ffn_post_g[l])

    return xt.reshape(batch, seq, d)
```

```python
import functools
import math

import jax
import jax.numpy as jnp
from jax import lax
from jax.experimental import pallas as pl
from jax.experimental.pallas import tpu as pltpu

NORM_EPS = 1e-6

DA_HEADS = 8
DA_QK_DIM = 64
DA_V_DIM = 2 * DA_QK_DIM
DA_WIDTH = DA_HEADS * DA_V_DIM
HG_HEADS = 8
HG_DK = 128
HG_DV = 128
HG_WIDTH = HG_HEADS * HG_DV
REL_BUCKETS = 32
REL_MAX_DIST = 128
CX_HEADS = 4
CX_HEAD_DIM = 128
CX_WIDTH = CX_HEADS * CX_HEAD_DIM

LANES = 128
SUBLANES = 8
MASK_VALUE = -1e30
LOG2E = math.log2(math.e)
VMEM_LIMIT = 56 * 1024 * 1024

MM_TM = 1024
MM_TN = 1024
OUT_TM = 512
CX_TM = 512
FFN_TM = 512
FFN_TH = 512
MIX_TILE = 512
HG_SB = 16
HG_BATCH = 4

_NT = (((1,), (1,)), ((), ()))
_TN = (((0,), (0,)), ((), ()))

f32 = jnp.float32
bf16 = jnp.bfloat16


def _rms(x, g):
    return x * lax.rsqrt(jnp.mean(x * x, axis=-1, keepdims=True) + NORM_EPS) * g


def _sigmoid(x):
    return 1.0 / (1.0 + jnp.exp(-x))


def _params(sem):
    return pltpu.CompilerParams(dimension_semantics=sem, vmem_limit_bytes=VMEM_LIMIT)


def _norm_matmul_kernel(x_ref, g_ref, w_ref, o_ref, h_ref, *, lead_blocks, lead_scale):
    @pl.when(pl.program_id(1) == 0)
    def _():
        h_ref[...] = _rms(x_ref[...], g_ref[...]).astype(bf16)

    y = jnp.dot(h_ref[...], w_ref[...], preferred_element_type=f32)
    if lead_blocks:
        y = y * jnp.where(pl.program_id(1) < lead_blocks, lead_scale, 1.0)
    if len(o_ref.shape) == 2:
        o_ref[...] = y.astype(o_ref.dtype)
    else:
        for c in range(o_ref.shape[0]):
            o_ref[c] = y[:, c * LANES:(c + 1) * LANES].astype(o_ref.dtype)


def _norm_matmul(x, g, w, layer, out_dtype, lead_cols=0, lead_scale=1.0, column_major=False):
    t, d = x.shape
    n = w.shape[2]
    tm, tn = min(MM_TM, t), min(MM_TN, n)
    assert t % tm == 0 and n % tn == 0 and lead_cols % tn == 0 and tn % LANES == 0
    if column_major:
        out_spec = pl.BlockSpec((tn // LANES, tm, LANES), lambda i, j: (j, i, 0))
        out_shape = jax.ShapeDtypeStruct((n // LANES, t, LANES), out_dtype)
    else:
        out_spec = pl.BlockSpec((tm, tn), lambda i, j: (i, j))
        out_shape = jax.ShapeDtypeStruct((t, n), out_dtype)
    return pl.pallas_call(
        functools.partial(_norm_matmul_kernel, lead_blocks=lead_cols // tn, lead_scale=lead_scale),
        grid=(t // tm, n // tn),
        in_specs=[
            pl.BlockSpec((tm, d), lambda i, j: (i, 0)),
            pl.BlockSpec((1, d), lambda i, j: (0, 0)),
            pl.BlockSpec((None, d, tn), lambda i, j: (layer, 0, j)),
        ],
        out_specs=out_spec,
        out_shape=out_shape,
        scratch_shapes=[pltpu.VMEM((tm, d), bf16)],
        compiler_params=_params(("parallel", "arbitrary")),
        name="norm_matmul",
    )(x, g.reshape(1, d), w)


def _mix_out_kernel(a1_ref, a2_ref, w1_ref, w2_ref, g_ref, x_ref, o_ref):
    heads = lambda ref: jnp.concatenate([ref[h] for h in range(ref.shape[0])], axis=-1)
    y = jnp.dot(heads(a1_ref), w1_ref[...], preferred_element_type=f32)
    y = y + jnp.dot(heads(a2_ref), w2_ref[...], preferred_element_type=f32)
    o_ref[...] = x_ref[...] + _rms(y, g_ref[...])


def _mix_out(a1, a2, w, layer, g, x):
    t, d = x.shape
    k1, k2 = a1.shape[0] * a1.shape[2], a2.shape[0] * a2.shape[2]
    assert k1 == k2 and w.shape[1] == k1 + k2
    tm = min(OUT_TM, t)
    assert t % tm == 0
    return pl.pallas_call(
        _mix_out_kernel,
        grid=(t // tm,),
        in_specs=[
            pl.BlockSpec((a1.shape[0], tm, a1.shape[2]), lambda i: (0, i, 0)),
            pl.BlockSpec((a2.shape[0], tm, a2.shape[2]), lambda i: (0, i, 0)),
            pl.BlockSpec((None, k1, d), lambda i: (layer, 0, 0)),
            pl.BlockSpec((None, k2, d), lambda i: (layer, 1, 0)),
            pl.BlockSpec((1, d), lambda i: (0, 0)),
            pl.BlockSpec((tm, d), lambda i: (i, 0)),
        ],
        out_specs=pl.BlockSpec((tm, d), lambda i: (i, 0)),
        out_shape=jax.ShapeDtypeStruct((t, d), f32),
        compiler_params=_params(("parallel",)),
        name="mix_out",
    )(a1, a2, w, w, g.reshape(1, d), x)


def _t5_causal_bucket(dist):
    n = jnp.maximum(dist, 0)
    max_exact = REL_BUCKETS // 2
    nf = jnp.maximum(n, 1).astype(f32)
    large = max_exact + (jnp.log(nf / max_exact) / math.log(REL_MAX_DIST / max_exact)
                         * (REL_BUCKETS - max_exact)).astype(jnp.int32)
    large = jnp.minimum(large, REL_BUCKETS - 1)
    return jnp.where(n < max_exact, n, large)


def _toeplitz(vec, t):
    h = vec.shape[0]
    r = jnp.concatenate([vec[:, ::-1], jnp.zeros((h, 1), vec.dtype)], axis=1)
    g = jnp.tile(r, (1, t))[:, :t * (2 * t - 1)].reshape(h, t, 2 * t - 1)
    return g[:, :, t - 1:]


def _da_bias_tiles(rel_bias, tq):
    d = jnp.arange(-(tq - 1), tq)
    b0 = jnp.where((d >= 0)[:, None], rel_bias[_t5_causal_bucket(d)].astype(f32), MASK_VALUE)
    b1 = rel_bias[_t5_causal_bucket(d + tq)].astype(f32)
    return jnp.stack([_toeplitz(b0.T, tq), _toeplitz(b1.T, tq)], axis=1)


def _hg_step(f_ref, q_ref, i_ref, g_ref, lbl_ref, on_ref, o_ref, st_ref, kf_ref, bf_ref, vf_ref,
             *, layer, tc, sb):
    groups = tc // sb

    lg = lbl_ref[...]
    pe = jnp.exp(lg - jnp.max(lg, axis=0, keepdims=True))
    lb = jnp.zeros((1, lg.shape[1]), f32)
    for r in range(1, layer + 1):
        lb = lb + pe[r:r + 1]
    lb = lb / jnp.sum(pe, axis=0, keepdims=True)

    f = lb + (1.0 - lb) * _sigmoid(f_ref[...].astype(f32))
    k = 1.0 - f
    b = jnp.log2(f)
    r = lax.broadcasted_iota(jnp.int32, b.shape, 0) % sb
    sh = 1
    while sh < sb:
        b = b + jnp.where(r >= sh, pltpu.roll(b, sh, axis=0), 0.0)
        sh *= 2
    q = q_ref[...].astype(f32)
    v = i_ref[...].astype(f32)
    kf_ref[...] = k
    bf_ref[...] = b
    vf_ref[...] = v

    def source_row(ref, s, g0):
        return jnp.stack([jnp.broadcast_to(ref[sb * g + s:sb * g + s + 1, :], (SUBLANES, ref.shape[1]))
                          for g in range(g0, g0 + HG_BATCH)])

    def pieces(x):
        lo = jnp.stack([x[sb * g:sb * g + SUBLANES] for g in range(groups)])
        hi = jnp.stack([x[sb * g + SUBLANES:sb * (g + 1)] for g in range(groups)])
        return lo, hi

    q_lo, q_hi = pieces(q)
    k_lo, k_hi = pieces(k)
    b_lo, b_hi = pieces(b)
    b_end = b_hi[:, SUBLANES - 1:SUBLANES, :]

    qd_lo, qd_hi = q_lo * jnp.exp2(b_lo), q_hi * jnp.exp2(b_hi)
    kd_lo, kd_hi = k_lo * jnp.exp2(b_end - b_lo), k_hi * jnp.exp2(b_end - b_hi)
    block_decay = jnp.exp2(b_end[:, 0, :])
    vb = v.astype(bf16)
    increments = []
    for g in range(groups):
        kd = jnp.concatenate([kd_lo[g], kd_hi[g]], axis=0).astype(bf16)
        increments.append(lax.dot_general(vb[sb * g:sb * (g + 1)], kd, _TN, preferred_element_type=f32))
    st = st_ref[...]
    states = []
    for g in range(groups):
        states.append(st.astype(bf16))
        st = block_decay[g:g + 1] * st + increments[g]
    st_ref[...] = st
    o_state = []
    for g in range(groups):
        qd = jnp.concatenate([qd_lo[g], qd_hi[g]], axis=0).astype(bf16)
        o_state.append(lax.dot_general(qd, states[g], _NT, preferred_element_type=f32))

    trow = lax.broadcasted_iota(jnp.int32, (1, SUBLANES, 1), 1)

    def contrib(q_p, b_p, ks, bs, vs, diag_row):
        d = b_p - bs
        if diag_row is not None:
            d = jnp.minimum(d, 0.0)
        a = jnp.sum(q_p * ks * jnp.exp2(d), axis=-1, keepdims=True)
        if diag_row is not None:
            a = jnp.where(trow >= diag_row, a, 0.0)
        return a * vs

    o_blocks = []
    for g0 in range(0, groups, HG_BATCH):
        batch = slice(g0, g0 + HG_BATCH)
        ql, qh, bl, bh = q_lo[batch], q_hi[batch], b_lo[batch], b_hi[batch]
        o_lo = jnp.zeros(ql.shape, f32)
        o_hi = jnp.zeros(qh.shape, f32)
        for s in range(SUBLANES):
            ks, bs, vs = source_row(kf_ref, s, g0), source_row(bf_ref, s, g0), source_row(vf_ref, s, g0)
            o_lo = o_lo + contrib(ql, bl, ks, bs, vs, s)
            o_hi = o_hi + contrib(qh, bh, ks, bs, vs, None)
        for s in range(SUBLANES, sb):
            ks, bs, vs = source_row(kf_ref, s, g0), source_row(bf_ref, s, g0), source_row(vf_ref, s, g0)
            o_hi = o_hi + contrib(qh, bh, ks, bs, vs, s - SUBLANES)
        o_blocks += [jnp.concatenate([o_lo[g], o_hi[g]], axis=0) + o_state[g0 + g] for g in range(HG_BATCH)]
    o = jnp.concatenate(o_blocks, axis=0)

    gate = g_ref[...].astype(f32)
    o_ref[...] = (_rms(o, on_ref[...]) * (gate * _sigmoid(gate))).astype(o_ref.dtype)


def _mixer_kernel(q_ref, k_ref, v_ref, bias_ref, far_ref, lq1_ref, lk1_ref, lq2_ref, lk2_ref, sg_ref,
                  hf_ref, hq_ref, hi_ref, hg_ref, lbl_ref, on_ref,
                  oda_ref, ohg_ref,
                  m_ref, l_ref, acc_ref, p_ref, alpha_ref, st_ref, kf_ref, bf_ref, vf_ref,
                  *, tq, lam_init, layer, sb):
    i = pl.program_id(2)

    @pl.when(i == 0)
    def _():
        st_ref[...] = jnp.zeros(st_ref.shape, f32)

    q = q_ref[...]
    lane = lax.broadcasted_iota(jnp.int32, q.shape, 1)
    zero = jnp.zeros_like(q)
    q_maps = (jnp.where(lane < DA_QK_DIM, q, zero), jnp.where(lane >= DA_QK_DIM, q, zero))

    m_ref[...] = jnp.full(m_ref.shape, MASK_VALUE, f32)
    l_ref[...] = jnp.zeros(l_ref.shape, f32)
    acc_ref[...] = jnp.zeros(acc_ref.shape, f32)

    def rows(j):
        return pl.ds(pl.multiple_of(j * tq, tq), tq)

    def unit(pv_block, pv_slot, qk_block, bias, qk_slot):
        if qk_block is not None:
            kb = k_ref[rows(qk_block), :]
            scores = [lax.dot_general(q_maps[mi], kb, _NT, preferred_element_type=f32) for mi in range(2)]
        if pv_block is not None:
            vb = v_ref[rows(pv_block), :]
            for mi in range(2):
                acc_ref[mi] = alpha_ref[pv_slot, mi] * acc_ref[mi] + jnp.dot(
                    p_ref[pv_slot, mi], vb, preferred_element_type=f32)
        if qk_block is not None:
            for mi in range(2):
                s = scores[mi] if bias is None else scores[mi] + bias
                m_prev = m_ref[mi]
                m_new = jnp.maximum(m_prev, jnp.max(s, axis=-1, keepdims=True))
                alpha = jnp.exp2(m_prev - m_new)
                p = jnp.exp2(s - jnp.tile(m_new, (1, tq // LANES)))
                l_ref[mi] = alpha * l_ref[mi] + jnp.sum(p, axis=-1, keepdims=True)
                m_ref[mi] = m_new
                alpha_ref[qk_slot, mi] = alpha
                p_ref[qk_slot, mi] = p.astype(p_ref.dtype)

    unit(None, None, i, bias_ref[0], 0)

    _hg_step(hf_ref, hq_ref, hi_ref, hg_ref, lbl_ref, on_ref, ohg_ref, st_ref, kf_ref, bf_ref, vf_ref,
             layer=layer, tc=tq, sb=sb)

    @pl.when(i == 0)
    def _():
        unit(i, 0, None, None, None)

    @pl.when(i >= 1)
    def _():
        unit(i, 0, i - 1, bias_ref[1], 1)
        m_ref[...] = m_ref[...] - far_ref[...]
        n_far = i - 1

        def before(k):
            return jnp.where(k == 0, i - 1, k - 1)

        def far_pair(t, carry):
            unit(before(2 * t), 1, 2 * t, None, 0)
            unit(2 * t, 0, 2 * t + 1, None, 1)
            return carry

        lax.fori_loop(0, n_far // 2, far_pair, 0)

        @pl.when(n_far % 2 == 1)
        def _():
            unit(before(n_far - 1), 1, n_far - 1, None, 0)
            unit(n_far - 1, 0, None, None, None)

        @pl.when(n_far % 2 == 0)
        def _():
            unit(before(n_far), 1, None, None, None)

    lam = (jnp.exp(jnp.sum(lq1_ref[...] * lk1_ref[...], axis=-1, keepdims=True))
           - jnp.exp(jnp.sum(lq2_ref[...] * lk2_ref[...], axis=-1, keepdims=True)) + lam_init)
    o = acc_ref[0] / l_ref[0] - lam * (acc_ref[1] / l_ref[1])
    oda_ref[...] = (_rms(o, sg_ref[...]) * (1.0 - lam_init)).astype(oda_ref.dtype)


def _token_mixer(proj, rel_bias, lq1, lk1, lq2, lk2, subln_g, lb_logits, onorm_g, *, batch, seq, layer, lam_init):
    tq = min(MIX_TILE, seq)
    assert seq % tq == 0 and tq >= REL_MAX_DIST and tq % HG_SB == 0 and HG_SB == 2 * SUBLANES
    assert DA_HEADS == HG_HEADS and DA_V_DIM == HG_DK == HG_DV == LANES
    nq = seq // tq
    depth = lb_logits.shape[0]
    bias = _da_bias_tiles(rel_bias * LOG2E, tq)
    far = jnp.broadcast_to((rel_bias[REL_BUCKETS - 1].astype(f32) * LOG2E)[:, None, None], (DA_HEADS, 1, LANES))
    vec = lambda a: a.reshape(1, DA_QK_DIM).astype(f32)
    lam_spec = pl.BlockSpec((1, DA_QK_DIM), lambda b, h, i: (0, 0))
    tile = lambda group: pl.BlockSpec((None, tq, LANES), lambda b, h, i: (group * DA_HEADS + h, b * nq + i, 0))
    whole = lambda group: pl.BlockSpec((None, seq, LANES), lambda b, h, i: (group * DA_HEADS + h, b, 0))
    gain = pl.BlockSpec((1, LANES), lambda b, h, i: (0, 0))
    out = pl.BlockSpec((None, tq, LANES), lambda b, h, i: (h, b * nq + i, 0))
    return pl.pallas_call(
        functools.partial(_mixer_kernel, tq=tq, lam_init=lam_init, layer=layer, sb=HG_SB),
        grid=(batch, DA_HEADS, nq),
        in_specs=[
            tile(0), whole(1), whole(2),
            pl.BlockSpec((None, 2, tq, tq), lambda b, h, i: (h, 0, 0, 0)),
            pl.BlockSpec((None, 1, LANES), lambda b, h, i: (h, 0, 0)),
            lam_spec, lam_spec, lam_spec, lam_spec, gain,
            tile(3), tile(4), tile(5), tile(6),
            pl.BlockSpec((depth, LANES), lambda b, h, i: (0, h)),
            gain,
        ],
        out_specs=[out, out],
        out_shape=[jax.ShapeDtypeStruct((DA_HEADS, batch * seq, DA_V_DIM), bf16),
                   jax.ShapeDtypeStruct((HG_HEADS, batch * seq, HG_DV), bf16)],
        scratch_shapes=[
            pltpu.VMEM((2, tq, LANES), f32),
            pltpu.VMEM((2, tq, LANES), f32),
            pltpu.VMEM((2, tq, DA_V_DIM), f32),
            pltpu.VMEM((2, 2, tq, tq), bf16),
            pltpu.VMEM((2, 2, tq, LANES), f32),
            pltpu.VMEM((HG_DV, HG_DK), f32),
            pltpu.VMEM((tq, HG_DK), f32),
            pltpu.VMEM((tq, HG_DK), f32),
            pltpu.VMEM((tq, HG_DV), f32),
        ],
        compiler_params=_params(("parallel", "parallel", "arbitrary")),
        name="token_mixer",
    )(proj, proj, proj, bias, far, vec(lq1), vec(lk1), vec(lq2), vec(lk2), subln_g.reshape(1, DA_V_DIM),
      proj, proj, proj, proj, lb_logits.astype(f32), onorm_g.reshape(1, HG_DV))


def _cross_kernel(x_ref, gpre_ref, wq_ref, kv_ref, wo_ref, gpost_ref, o_ref):
    x = x_ref[...]
    h = _rms(x, gpre_ref[...]).astype(bf16)
    q = jnp.dot(h, wq_ref[...], preferred_element_type=f32).astype(bf16)
    outs = []
    for hd in range(CX_HEADS):
        cols = slice(hd * CX_HEAD_DIM, (hd + 1) * CX_HEAD_DIM)
        kh = kv_ref[:, cols]
        vh = kv_ref[:, CX_WIDTH + hd * CX_HEAD_DIM:CX_WIDTH + (hd + 1) * CX_HEAD_DIM]
        s = lax.dot_general(q[:, cols], kh, _NT, preferred_element_type=f32) * CX_HEAD_DIM ** -0.5
        p = jnp.exp(s - jnp.max(s, axis=-1, keepdims=True))
        oh = jnp.dot(p.astype(bf16), vh, preferred_element_type=f32)
        outs.append((oh / jnp.sum(p, axis=-1, keepdims=True)).astype(bf16))
    o = jnp.concatenate(outs, axis=-1)
    y = jnp.dot(o, wo_ref[...], preferred_element_type=f32)
    o_ref[...] = x + _rms(y, gpost_ref[...])


def _cross_attention(x, kv, wq, wo, layer, gpre, gpost, *, batch, seq, mem_len):
    t, d = x.shape
    tm = min(CX_TM, seq)
    assert seq % tm == 0
    per_batch = seq // tm
    return pl.pallas_call(
        _cross_kernel,
        grid=(t // tm,),
        in_specs=[
            pl.BlockSpec((tm, d), lambda i: (i, 0)),
            pl.BlockSpec((1, d), lambda i: (0, 0)),
            pl.BlockSpec((None, d, CX_WIDTH), lambda i: (layer, 0, 0)),
            pl.BlockSpec((mem_len, 2 * CX_WIDTH), lambda i: (i // per_batch, 0)),
            pl.BlockSpec((None, CX_WIDTH, d), lambda i: (layer, 0, 0)),
            pl.BlockSpec((1, d), lambda i: (0, 0)),
        ],
        out_specs=pl.BlockSpec((tm, d), lambda i: (i, 0)),
        out_shape=jax.ShapeDtypeStruct((t, d), f32),
        compiler_params=_params(("parallel",)),
        name="cross_attention",
    )(x, gpre.reshape(1, d), wq, kv, wo, gpost.reshape(1, d))


def _ffn_kernel(x_ref, gpre_ref, wg_ref, wu_ref, wo_ref, gpost_ref, o_ref, h_ref, acc_ref):
    j = pl.program_id(1)

    @pl.when(j == 0)
    def _():
        h_ref[...] = _rms(x_ref[...], gpre_ref[...]).astype(bf16)
        acc_ref[...] = jnp.zeros(acc_ref.shape, f32)

    h = h_ref[...]
    gate = jnp.dot(h, wg_ref[...], preferred_element_type=f32)
    up = jnp.dot(h, wu_ref[...], preferred_element_type=f32)
    a = (gate * _sigmoid(gate) * up).astype(bf16)
    acc_ref[...] += jnp.dot(a, wo_ref[...], preferred_element_type=f32)

    @pl.when(j == pl.num_programs(1) - 1)
    def _():
        o_ref[...] = x_ref[...] + _rms(acc_ref[...], gpost_ref[...])


def _ffn(x, w_in, w_out, layer, gpre, gpost):
    t, d = x.shape
    hidden = w_out.shape[1]
    tm, th = min(FFN_TM, t), min(FFN_TH, hidden)
    assert t % tm == 0 and hidden % th == 0 and w_in.shape[2] == 2 * hidden
    nh = hidden // th
    return pl.pallas_call(
        _ffn_kernel,
        grid=(t // tm, nh),
        in_specs=[
            pl.BlockSpec((tm, d), lambda i, j: (i, 0)),
            pl.BlockSpec((1, d), lambda i, j: (0, 0)),
            pl.BlockSpec((None, d, th), lambda i, j: (layer, 0, j)),
            pl.BlockSpec((None, d, th), lambda i, j: (layer, 0, nh + j)),
            pl.BlockSpec((None, th, d), lambda i, j: (layer, j, 0)),
            pl.BlockSpec((1, d), lambda i, j: (0, 0)),
        ],
        out_specs=pl.BlockSpec((tm, d), lambda i, j: (i, 0)),
        out_shape=jax.ShapeDtypeStruct((t, d), f32),
        scratch_shapes=[pltpu.VMEM((tm, d), bf16), pltpu.VMEM((tm, d), f32)],
        compiler_params=_params(("parallel", "arbitrary")),
        name="ffn",
    )(x, gpre.reshape(1, d), w_in, w_in, w_out, gpost.reshape(1, d))


def kernel(x, mem, w_in, w_out, w_cq, w_ckv, w_co, w_ffn_in, w_ffn_out, mix_pre_g, mix_post_g, cross_pre_g, cross_post_g, mem_norm_g, ffn_pre_g, ffn_post_g, da_subln_g, hg_onorm_g, lambda_q1, lambda_k1, lambda_q2, lambda_k2, hg_lb_logits, rel_bias):
    batch, seq, d = x.shape
    mem_len = mem.shape[1]
    depth = w_in.shape[0]
    xt = x.reshape(batch * seq, d)
    memt = mem.reshape(batch * mem_len, d)
    w_in, w_out, w_cq, w_ckv, w_co, w_ffn_in, w_ffn_out = (
        w.astype(bf16) for w in (w_in, w_out, w_cq, w_ckv, w_co, w_ffn_in, w_ffn_out))

    for l in range(depth):
        lam_init = 0.8 - 0.6 * math.exp(-0.3 * l)
        proj = _norm_matmul(xt, mix_pre_g[l], w_in, l, bf16,
                            lead_cols=DA_WIDTH, lead_scale=DA_QK_DIM ** -0.5 * LOG2E, column_major=True)
        o_da, o_hg = _token_mixer(proj, rel_bias, lambda_q1[l], lambda_k1[l], lambda_q2[l], lambda_k2[l],
                                  da_subln_g[l], hg_lb_logits, hg_onorm_g[l],
                                  batch=batch, seq=seq, layer=l, lam_init=lam_init)
        xt = _mix_out(o_da, o_hg, w_out, l, mix_post_g[l], xt)
        kv = _norm_matmul(memt, mem_norm_g[l], w_ckv, l, bf16)
        xt = _cross_attention(xt, kv, w_cq, w_co, l, cross_pre_g[l], cross_post_g[l],
                              batch=batch, seq=seq, mem_len=mem_len)
        xt = _ffn(xt, w_ffn_in, w_ffn_out, l, ffn_pre_g[l], ffn_post_g[l])

    return xt.reshape(batch, seq, d)
```

```python
import functools
import math

import jax
import jax.numpy as jnp
from jax import lax
from jax.experimental import pallas as pl
from jax.experimental.pallas import tpu as pltpu

NORM_EPS = 1e-6

DA_HEADS = 8
DA_QK_DIM = 64
DA_V_DIM = 2 * DA_QK_DIM
DA_WIDTH = DA_HEADS * DA_V_DIM
HG_HEADS = 8
HG_DK = 128
HG_DV = 128
HG_WIDTH = HG_HEADS * HG_DV
REL_BUCKETS = 32
REL_MAX_DIST = 128
CX_HEADS = 4
CX_HEAD_DIM = 128
CX_WIDTH = CX_HEADS * CX_HEAD_DIM

LANES = 128
SUBLANES = 8
MASK_VALUE = -1e30
LOG2E = math.log2(math.e)
VMEM_LIMIT = 56 * 1024 * 1024

MM_TM = 1024
MM_TN = 1024
OUT_TM = 512
CX_TM = 512
FFN_TM = 512
FFN_TH = 512
MIX_TILE = 512
HG_SB = 16
HG_BATCH = 4

_NT = (((1,), (1,)), ((), ()))
_TN = (((0,), (0,)), ((), ()))

f32 = jnp.float32
bf16 = jnp.bfloat16


def _rms(x, g):
    return x * lax.rsqrt(jnp.mean(x * x, axis=-1, keepdims=True) + NORM_EPS) * g


def _sigmoid(x):
    return 1.0 / (1.0 + jnp.exp(-x))


def _alternate(*staged):
    live = list(staged)
    while live:
        for stage in tuple(live):
            if next(stage, live) is live:
                live.remove(stage)


def _params(sem):
    return pltpu.CompilerParams(dimension_semantics=sem, vmem_limit_bytes=VMEM_LIMIT)


def _norm_matmul_kernel(x_ref, g_ref, w_ref, o_ref, h_ref, *, lead_blocks, lead_scale):
    @pl.when(pl.program_id(1) == 0)
    def _():
        h_ref[...] = _rms(x_ref[...], g_ref[...]).astype(bf16)

    y = jnp.dot(h_ref[...], w_ref[...], preferred_element_type=f32)
    if lead_blocks:
        y = y * jnp.where(pl.program_id(1) < lead_blocks, lead_scale, 1.0)
    if len(o_ref.shape) == 2:
        o_ref[...] = y.astype(o_ref.dtype)
    else:
        for c in range(o_ref.shape[0]):
            o_ref[c] = y[:, c * LANES:(c + 1) * LANES].astype(o_ref.dtype)


def _norm_matmul(x, g, w, layer, out_dtype, lead_cols=0, lead_scale=1.0, column_major=False):
    t, d = x.shape
    n = w.shape[2]
    tm, tn = min(MM_TM, t), min(MM_TN, n)
    assert t % tm == 0 and n % tn == 0 and lead_cols % tn == 0 and tn % LANES == 0
    if column_major:
        out_spec = pl.BlockSpec((tn // LANES, tm, LANES), lambda i, j: (j, i, 0))
        out_shape = jax.ShapeDtypeStruct((n // LANES, t, LANES), out_dtype)
    else:
        out_spec = pl.BlockSpec((tm, tn), lambda i, j: (i, j))
        out_shape = jax.ShapeDtypeStruct((t, n), out_dtype)
    return pl.pallas_call(
        functools.partial(_norm_matmul_kernel, lead_blocks=lead_cols // tn, lead_scale=lead_scale),
        grid=(t // tm, n // tn),
        in_specs=[
            pl.BlockSpec((tm, d), lambda i, j: (i, 0)),
            pl.BlockSpec((1, d), lambda i, j: (0, 0)),
            pl.BlockSpec((None, d, tn), lambda i, j: (layer, 0, j)),
        ],
        out_specs=out_spec,
        out_shape=out_shape,
        scratch_shapes=[pltpu.VMEM((tm, d), bf16)],
        compiler_params=_params(("parallel", "arbitrary")),
        name="norm_matmul",
    )(x, g.reshape(1, d), w)


def _mix_out_kernel(a1_ref, a2_ref, w1_ref, w2_ref, g_ref, x_ref, o_ref):
    heads = lambda ref: jnp.concatenate([ref[h] for h in range(ref.shape[0])], axis=-1)
    y = jnp.dot(heads(a1_ref), w1_ref[...], preferred_element_type=f32)
    y = y + jnp.dot(heads(a2_ref), w2_ref[...], preferred_element_type=f32)
    o_ref[...] = x_ref[...] + _rms(y, g_ref[...])


def _mix_out(a1, a2, w, layer, g, x):
    t, d = x.shape
    k1, k2 = a1.shape[0] * a1.shape[2], a2.shape[0] * a2.shape[2]
    assert k1 == k2 and w.shape[1] == k1 + k2
    tm = min(OUT_TM, t)
    assert t % tm == 0
    return pl.pallas_call(
        _mix_out_kernel,
        grid=(t // tm,),
        in_specs=[
            pl.BlockSpec((a1.shape[0], tm, a1.shape[2]), lambda i: (0, i, 0)),
            pl.BlockSpec((a2.shape[0], tm, a2.shape[2]), lambda i: (0, i, 0)),
            pl.BlockSpec((None, k1, d), lambda i: (layer, 0, 0)),
            pl.BlockSpec((None, k2, d), lambda i: (layer, 1, 0)),
            pl.BlockSpec((1, d), lambda i: (0, 0)),
            pl.BlockSpec((tm, d), lambda i: (i, 0)),
        ],
        out_specs=pl.BlockSpec((tm, d), lambda i: (i, 0)),
        out_shape=jax.ShapeDtypeStruct((t, d), f32),
        compiler_params=_params(("parallel",)),
        name="mix_out",
    )(a1, a2, w, w, g.reshape(1, d), x)


def _t5_causal_bucket(dist):
    n = jnp.maximum(dist, 0)
    max_exact = REL_BUCKETS // 2
    nf = jnp.maximum(n, 1).astype(f32)
    large = max_exact + (jnp.log(nf / max_exact) / math.log(REL_MAX_DIST / max_exact)
                         * (REL_BUCKETS - max_exact)).astype(jnp.int32)
    large = jnp.minimum(large, REL_BUCKETS - 1)
    return jnp.where(n < max_exact, n, large)


def _toeplitz(vec, t):
    h = vec.shape[0]
    r = jnp.concatenate([vec[:, ::-1], jnp.zeros((h, 1), vec.dtype)], axis=1)
    g = jnp.tile(r, (1, t))[:, :t * (2 * t - 1)].reshape(h, t, 2 * t - 1)
    return g[:, :, t - 1:]


def _da_bias_tiles(rel_bias, tq):
    d = jnp.arange(-(tq - 1), tq)
    b0 = jnp.where((d >= 0)[:, None], rel_bias[_t5_causal_bucket(d)].astype(f32), MASK_VALUE)
    b1 = rel_bias[_t5_causal_bucket(d + tq)].astype(f32)
    return jnp.stack([_toeplitz(b0.T, tq), _toeplitz(b1.T, tq)], axis=1)


def _hg_step(f_ref, q_ref, i_ref, g_ref, lbl_ref, on_ref, o_ref, st_ref, kf_ref, bf_ref, vf_ref,
             *, layer, row0, tc, sb):
    groups = tc // sb
    piece = slice(row0, row0 + tc)

    lg = lbl_ref[...]
    pe = jnp.exp(lg - jnp.max(lg, axis=0, keepdims=True))
    lb = jnp.zeros((1, lg.shape[1]), f32)
    for r in range(1, layer + 1):
        lb = lb + pe[r:r + 1]
    lb = lb / jnp.sum(pe, axis=0, keepdims=True)

    f = lb + (1.0 - lb) * _sigmoid(f_ref[piece, :].astype(f32))
    k = 1.0 - f
    b = jnp.log2(f)
    r = lax.broadcasted_iota(jnp.int32, b.shape, 0) % sb
    sh = 1
    while sh < sb:
        b = b + jnp.where(r >= sh, pltpu.roll(b, sh, axis=0), 0.0)
        sh *= 2
    q = q_ref[piece, :].astype(f32)
    v = i_ref[piece, :].astype(f32)
    kf_ref[piece, :] = k
    bf_ref[piece, :] = b
    vf_ref[piece, :] = v

    def source_row(ref, s, g0):
        return jnp.stack([jnp.broadcast_to(ref[row0 + sb * g + s:row0 + sb * g + s + 1, :],
                                           (SUBLANES, ref.shape[1])) for g in range(g0, g0 + HG_BATCH)])

    def pieces(x):
        lo = jnp.stack([x[sb * g:sb * g + SUBLANES] for g in range(groups)])
        hi = jnp.stack([x[sb * g + SUBLANES:sb * (g + 1)] for g in range(groups)])
        return lo, hi

    q_lo, q_hi = pieces(q)
    k_lo, k_hi = pieces(k)
    b_lo, b_hi = pieces(b)
    b_end = b_hi[:, SUBLANES - 1:SUBLANES, :]

    qd_lo, qd_hi = q_lo * jnp.exp2(b_lo), q_hi * jnp.exp2(b_hi)
    kd_lo, kd_hi = k_lo * jnp.exp2(b_end - b_lo), k_hi * jnp.exp2(b_end - b_hi)
    block_decay = jnp.exp2(b_end[:, 0, :])
    vb = v.astype(bf16)
    increments = []
    for g in range(groups):
        kd = jnp.concatenate([kd_lo[g], kd_hi[g]], axis=0).astype(bf16)
        increments.append(lax.dot_general(vb[sb * g:sb * (g + 1)], kd, _TN, preferred_element_type=f32))
    st = st_ref[...]
    states = []
    for g in range(groups):
        states.append(st.astype(bf16))
        st = block_decay[g:g + 1] * st + increments[g]
    st_ref[...] = st
    o_state = []
    for g in range(groups):
        qd = jnp.concatenate([qd_lo[g], qd_hi[g]], axis=0).astype(bf16)
        o_state.append(lax.dot_general(qd, states[g], _NT, preferred_element_type=f32))

    trow = lax.broadcasted_iota(jnp.int32, (1, SUBLANES, 1), 1)

    def contrib(q_p, b_p, ks, bs, vs, diag_row):
        d = b_p - bs
        if diag_row is not None:
            d = jnp.where(trow >= diag_row, d, MASK_VALUE)
        a = jnp.sum(q_p * ks * jnp.exp2(d), axis=-1, keepdims=True)
        return a * vs

    o_blocks = []
    for g0 in range(0, groups, HG_BATCH):
        yield
        batch = slice(g0, g0 + HG_BATCH)
        ql, qh, bl, bh = q_lo[batch], q_hi[batch], b_lo[batch], b_hi[batch]
        o_lo = jnp.zeros(ql.shape, f32)
        o_hi = jnp.zeros(qh.shape, f32)
        for s in range(SUBLANES):
            ks, bs, vs = source_row(kf_ref, s, g0), source_row(bf_ref, s, g0), source_row(vf_ref, s, g0)
            o_lo = o_lo + contrib(ql, bl, ks, bs, vs, s)
            o_hi = o_hi + contrib(qh, bh, ks, bs, vs, None)
        for s in range(SUBLANES, sb):
            ks, bs, vs = source_row(kf_ref, s, g0), source_row(bf_ref, s, g0), source_row(vf_ref, s, g0)
            o_hi = o_hi + contrib(qh, bh, ks, bs, vs, s - SUBLANES)
        o_blocks += [jnp.concatenate([o_lo[g], o_hi[g]], axis=0) + o_state[g0 + g] for g in range(HG_BATCH)]
    o = jnp.concatenate(o_blocks, axis=0)
    yield

    gate = g_ref[piece, :].astype(f32)
    o_ref[piece, :] = (_rms(o, on_ref[...]) * (gate * _sigmoid(gate))).astype(o_ref.dtype)


def _mixer_kernel(q_ref, k_ref, v_ref, bias_ref, far_ref, lq1_ref, lk1_ref, lq2_ref, lk2_ref, sg_ref,
                  hf_ref, hq_ref, hi_ref, hg_ref, lbl_ref, on_ref,
                  oda_ref, ohg_ref,
                  m_ref, l_ref, acc_ref, p_ref, alpha_ref, st_ref, kf_ref, bf_ref, vf_ref,
                  *, tq, lam_init, layer, sb):
    i = pl.program_id(2)

    @pl.when(i == 0)
    def _():
        st_ref[...] = jnp.zeros(st_ref.shape, f32)

    q = q_ref[...]
    lane = lax.broadcasted_iota(jnp.int32, q.shape, 1)
    zero = jnp.zeros_like(q)
    q_maps = (jnp.where(lane < DA_QK_DIM, q, zero), jnp.where(lane >= DA_QK_DIM, q, zero))

    m_ref[...] = jnp.full(m_ref.shape, MASK_VALUE, f32)
    l_ref[...] = jnp.zeros(l_ref.shape, f32)
    acc_ref[...] = jnp.zeros(acc_ref.shape, f32)

    def rows(j):
        return pl.ds(pl.multiple_of(j * tq, tq), tq)

    whole_tile = ((0, tq, tq),)
    diagonal_tile = ((0, tq // 2, tq // 2), (tq // 2, tq // 2, tq))

    def unit(pv_block, pv_slot, qk_block, bias_tile, qk_slot, parts=whole_tile):
        if qk_block is not None:
            kb = k_ref[rows(qk_block), :]
            scores = [[lax.dot_general(q_maps[mi][r0:r0 + nr], kb[:nk], _NT, preferred_element_type=f32)
                       for mi in range(2)] for r0, nr, nk in parts]
        if pv_block is not None:
            vb = v_ref[rows(pv_block), :]
            for mi in range(2):
                acc_ref[mi] = alpha_ref[pv_slot, mi] * acc_ref[mi] + jnp.dot(
                    p_ref[pv_slot, mi], vb, preferred_element_type=f32)
        if qk_block is not None:
            for (r0, nr, nk), s_maps in zip(parts, scores):
                rs = slice(r0, r0 + nr)
                for mi in range(2):
                    yield
                    s = s_maps[mi] if bias_tile is None else s_maps[mi] + bias_ref[bias_tile, rs, :nk]
                    m_prev = m_ref[mi, rs]
                    m_new = jnp.maximum(m_prev, jnp.max(s, axis=-1, keepdims=True))
                    alpha = jnp.exp2(m_prev - m_new)
                    p = jnp.exp2(s - jnp.tile(m_new, (1, nk // LANES)))
                    l_ref[mi, rs] = alpha * l_ref[mi, rs] + jnp.sum(p, axis=-1, keepdims=True)
                    m_ref[mi, rs] = m_new
                    alpha_ref[qk_slot, mi, rs] = alpha
                    p_ref[qk_slot, mi, rs, :nk] = p.astype(p_ref.dtype)
                    if nk < tq:
                        p_ref[qk_slot, mi, rs, nk:] = jnp.zeros((nr, tq - nk), p_ref.dtype)

    hg_half = functools.partial(_hg_step, hf_ref, hq_ref, hi_ref, hg_ref, lbl_ref, on_ref, ohg_ref, st_ref,
                                kf_ref, bf_ref, vf_ref, layer=layer, tc=tq // 2, sb=sb)
    _alternate(unit(None, None, i, 0, 0, diagonal_tile), hg_half(row0=0))

    @pl.when(i == 0)
    def _():
        _alternate(unit(i, 0, None, None, None), hg_half(row0=tq // 2))

    @pl.when(i >= 1)
    def _():
        _alternate(unit(i, 0, i - 1, 1, 1), hg_half(row0=tq // 2))
        m_ref[...] = m_ref[...] - far_ref[...]
        n_far = i - 1

        def before(k):
            return jnp.where(k == 0, i - 1, k - 1)

        def far_pair(t, carry):
            _alternate(unit(before(2 * t), 1, 2 * t, None, 0))
            _alternate(unit(2 * t, 0, 2 * t + 1, None, 1))
            return carry

        lax.fori_loop(0, n_far // 2, far_pair, 0)

        @pl.when(n_far % 2 == 1)
        def _():
            _alternate(unit(before(n_far - 1), 1, n_far - 1, None, 0))
            _alternate(unit(n_far - 1, 0, None, None, None))

        @pl.when(n_far % 2 == 0)
        def _():
            _alternate(unit(before(n_far), 1, None, None, None))

    lam = (jnp.exp(jnp.sum(lq1_ref[...] * lk1_ref[...], axis=-1, keepdims=True))
           - jnp.exp(jnp.sum(lq2_ref[...] * lk2_ref[...], axis=-1, keepdims=True)) + lam_init)
    o = acc_ref[0] / l_ref[0] - lam * (acc_ref[1] / l_ref[1])
    oda_ref[...] = (_rms(o, sg_ref[...]) * (1.0 - lam_init)).astype(oda_ref.dtype)


def _token_mixer(proj, rel_bias, lq1, lk1, lq2, lk2, subln_g, lb_logits, onorm_g, *, batch, seq, layer, lam_init):
    tq = min(MIX_TILE, seq)
    assert seq % tq == 0 and tq >= REL_MAX_DIST and tq % (2 * LANES) == 0 and HG_SB == 2 * SUBLANES
    assert DA_HEADS == HG_HEADS and DA_V_DIM == HG_DK == HG_DV == LANES
    nq = seq // tq
    depth = lb_logits.shape[0]
    bias = _da_bias_tiles(rel_bias * LOG2E, tq)
    far = jnp.broadcast_to((rel_bias[REL_BUCKETS - 1].astype(f32) * LOG2E)[:, None, None], (DA_HEADS, 1, LANES))
    vec = lambda a: a.reshape(1, DA_QK_DIM).astype(f32)
    lam_spec = pl.BlockSpec((1, DA_QK_DIM), lambda b, h, i: (0, 0))
    tile = lambda group: pl.BlockSpec((None, tq, LANES), lambda b, h, i: (group * DA_HEADS + h, b * nq + i, 0))
    whole = lambda group: pl.BlockSpec((None, seq, LANES), lambda b, h, i: (group * DA_HEADS + h, b, 0))
    gain = pl.BlockSpec((1, LANES), lambda b, h, i: (0, 0))
    out = pl.BlockSpec((None, tq, LANES), lambda b, h, i: (h, b * nq + i, 0))
    return pl.pallas_call(
        functools.partial(_mixer_kernel, tq=tq, lam_init=lam_init, layer=layer, sb=HG_SB),
        grid=(batch, DA_HEADS, nq),
        in_specs=[
            tile(0), whole(1), whole(2),
            pl.BlockSpec((None, 2, tq, tq), lambda b, h, i: (h, 0, 0, 0)),
            pl.BlockSpec((None, 1, LANES), lambda b, h, i: (h, 0, 0)),
            lam_spec, lam_spec, lam_spec, lam_spec, gain,
            tile(3), tile(4), tile(5), tile(6),
            pl.BlockSpec((depth, LANES), lambda b, h, i: (0, h)),
            gain,
        ],
        out_specs=[out, out],
        out_shape=[jax.ShapeDtypeStruct((DA_HEADS, batch * seq, DA_V_DIM), bf16),
                   jax.ShapeDtypeStruct((HG_HEADS, batch * seq, HG_DV), bf16)],
        scratch_shapes=[
            pltpu.VMEM((2, tq, LANES), f32),
            pltpu.VMEM((2, tq, LANES), f32),
            pltpu.VMEM((2, tq, DA_V_DIM), f32),
            pltpu.VMEM((2, 2, tq, tq), bf16),
            pltpu.VMEM((2, 2, tq, LANES), f32),
            pltpu.VMEM((HG_DV, HG_DK), f32),
            pltpu.VMEM((tq, HG_DK), f32),
            pltpu.VMEM((tq, HG_DK), f32),
            pltpu.VMEM((tq, HG_DV), f32),
        ],
        compiler_params=_params(("parallel", "parallel", "arbitrary")),
        name="token_mixer",
    )(proj, proj, proj, bias, far, vec(lq1), vec(lk1), vec(lq2), vec(lk2), subln_g.reshape(1, DA_V_DIM),
      proj, proj, proj, proj, lb_logits.astype(f32), onorm_g.reshape(1, HG_DV))


def _cross_kernel(x_ref, gpre_ref, wq_ref, kv_ref, wo_ref, gpost_ref, o_ref):
    x = x_ref[...]
    h = _rms(x, gpre_ref[...]).astype(bf16)
    q = jnp.dot(h, wq_ref[...], preferred_element_type=f32).astype(bf16)
    outs = []
    for hd in range(CX_HEADS):
        cols = slice(hd * CX_HEAD_DIM, (hd + 1) * CX_HEAD_DIM)
        kh = kv_ref[:, cols]
        vh = kv_ref[:, CX_WIDTH + hd * CX_HEAD_DIM:CX_WIDTH + (hd + 1) * CX_HEAD_DIM]
        s = lax.dot_general(q[:, cols], kh, _NT, preferred_element_type=f32) * CX_HEAD_DIM ** -0.5
        p = jnp.exp(s - jnp.max(s, axis=-1, keepdims=True))
        oh = jnp.dot(p.astype(bf16), vh, preferred_element_type=f32)
        outs.append((oh / jnp.sum(p, axis=-1, keepdims=True)).astype(bf16))
    o = jnp.concatenate(outs, axis=-1)
    y = jnp.dot(o, wo_ref[...], preferred_element_type=f32)
    o_ref[...] = x + _rms(y, gpost_ref[...])


def _cross_attention(x, kv, wq, wo, layer, gpre, gpost, *, batch, seq, mem_len):
    t, d = x.shape
    tm = min(CX_TM, seq)
    assert seq % tm == 0
    per_batch = seq // tm
    return pl.pallas_call(
        _cross_kernel,
        grid=(t // tm,),
        in_specs=[
            pl.BlockSpec((tm, d), lambda i: (i, 0)),
            pl.BlockSpec((1, d), lambda i: (0, 0)),
            pl.BlockSpec((None, d, CX_WIDTH), lambda i: (layer, 0, 0)),
            pl.BlockSpec((mem_len, 2 * CX_WIDTH), lambda i: (i // per_batch, 0)),
            pl.BlockSpec((None, CX_WIDTH, d), lambda i: (layer, 0, 0)),
            pl.BlockSpec((1, d), lambda i: (0, 0)),
        ],
        out_specs=pl.BlockSpec((tm, d), lambda i: (i, 0)),
        out_shape=jax.ShapeDtypeStruct((t, d), f32),
        compiler_params=_params(("parallel",)),
        name="cross_attention",
    )(x, gpre.reshape(1, d), wq, kv, wo, gpost.reshape(1, d))


def _ffn_kernel(x_ref, gpre_ref, wg_ref, wu_ref, wo_ref, gpost_ref, o_ref, h_ref, acc_ref):
    j = pl.program_id(1)

    @pl.when(j == 0)
    def _():
        h_ref[...] = _rms(x_ref[...], gpre_ref[...]).astype(bf16)
        acc_ref[...] = jnp.zeros(acc_ref.shape, f32)

    h = h_ref[...]
    gate = jnp.dot(h, wg_ref[...], preferred_element_type=f32)
    up = jnp.dot(h, wu_ref[...], preferred_element_type=f32)
    a = (gate * _sigmoid(gate) * up).astype(bf16)
    acc_ref[...] += jnp.dot(a, wo_ref[...], preferred_element_type=f32)

    @pl.when(j == pl.num_programs(1) - 1)
    def _():
        o_ref[...] = x_ref[...] + _rms(acc_ref[...], gpost_ref[...])


def _ffn(x, w_in, w_out, layer, gpre, gpost):
    t, d = x.shape
    hidden = w_out.shape[1]
    tm, th = min(FFN_TM, t), min(FFN_TH, hidden)
    assert t % tm == 0 and hidden % th == 0 and w_in.shape[2] == 2 * hidden
    nh = hidden // th
    return pl.pallas_call(
        _ffn_kernel,
        grid=(t // tm, nh),
        in_specs=[
            pl.BlockSpec((tm, d), lambda i, j: (i, 0)),
            pl.BlockSpec((1, d), lambda i, j: (0, 0)),
            pl.BlockSpec((None, d, th), lambda i, j: (layer, 0, j)),
            pl.BlockSpec((None, d, th), lambda i, j: (layer, 0, nh + j)),
            pl.BlockSpec((None, th, d), lambda i, j: (layer, j, 0)),
            pl.BlockSpec((1, d), lambda i, j: (0, 0)),
        ],
        out_specs=pl.BlockSpec((tm, d), lambda i, j: (i, 0)),
        out_shape=jax.ShapeDtypeStruct((t, d), f32),
        scratch_shapes=[pltpu.VMEM((tm, d), bf16), pltpu.VMEM((tm, d), f32)],
        compiler_params=_params(("parallel", "arbitrary")),
        name="ffn",
    )(x, gpre.reshape(1, d), w_in, w_in, w_out, gpost.reshape(1, d))


def kernel(x, mem, w_in, w_out, w_cq, w_ckv, w_co, w_ffn_in, w_ffn_out, mix_pre_g, mix_post_g, cross_pre_g, cross_post_g, mem_norm_g, ffn_pre_g, ffn_post_g, da_subln_g, hg_onorm_g, lambda_q1, lambda_k1, lambda_q2, lambda_k2, hg_lb_logits, rel_bias):
    batch, seq, d = x.shape
    mem_len = mem.shape[1]
    depth = w_in.shape[0]
    xt = x.reshape(batch * seq, d)
    memt = mem.reshape(batch * mem_len, d)
    w_in, w_out, w_cq, w_ckv, w_co, w_ffn_in, w_ffn_out = (
        w.astype(bf16) for w in (w_in, w_out, w_cq, w_ckv, w_co, w_ffn_in, w_ffn_out))

    for l in range(depth):
        lam_init = 0.8 - 0.6 * math.exp(-0.3 * l)
        proj = _norm_matmul(xt, mix_pre_g[l], w_in, l, bf16,
                            lead_cols=DA_WIDTH, lead_scale=DA_QK_DIM ** -0.5 * LOG2E, column_major=True)
        o_da, o_hg = _token_mixer(proj, rel_bias, lambda_q1[l], lambda_k1[l], lambda_q2[l], lambda_k2[l],
                                  da_subln_g[l], hg_lb_logits, hg_onorm_g[l],
                                  batch=batch, seq=seq, layer=l, lam_init=lam_init)
        xt = _mix_out(o_da, o_hg, w_out, l, mix_post_g[l], xt)
        kv = _norm_matmul(memt, mem_norm_g[l], w_ckv, l, bf16)
        xt = _cross_attention(xt, kv, w_cq, w_co, l, cross_pre_g[l], cross_post_g[l],
                              batch=batch, seq=seq, mem_len=mem_len)
        xt = _ffn(xt, w_ffn_in, w_ffn_out, l, ffn_pre_g[l], ffn_post_g[l])

    return xt.reshape(batch, seq, d)
```

```python
import functools
import math

import jax
import jax.numpy as jnp
from jax import lax
from jax.experimental import pallas as pl
from jax.experimental.pallas import tpu as pltpu

NORM_EPS = 1e-6

DA_HEADS = 8
DA_QK_DIM = 64
DA_V_DIM = 2 * DA_QK_DIM
DA_WIDTH = DA_HEADS * DA_V_DIM
HG_HEADS = 8
HG_DK = 128
HG_DV = 128
HG_WIDTH = HG_HEADS * HG_DV
REL_BUCKETS = 32
REL_MAX_DIST = 128
CX_HEADS = 4
CX_HEAD_DIM = 128
CX_WIDTH = CX_HEADS * CX_HEAD_DIM

LANES = 128
SUBLANES = 8
MASK_VALUE = -1e30
LOG2E = math.log2(math.e)
VMEM_LIMIT = 56 * 1024 * 1024

MM_TM = 1024
MM_TN = 1024
OUT_TM = 512
CX_TM = 512
FFN_TM = 512
FFN_TH = 512
MIX_TILE = 512
HG_SB = 16
HG_BATCH = 4

_NT = (((1,), (1,)), ((), ()))
_TN = (((0,), (0,)), ((), ()))

f32 = jnp.float32
bf16 = jnp.bfloat16


def _rms(x, g):
    return x * lax.rsqrt(jnp.mean(x * x, axis=-1, keepdims=True) + NORM_EPS) * g


def _sigmoid(x):
    return 1.0 / (1.0 + jnp.exp(-x))


def _alternate(*staged):
    live = list(staged)
    while live:
        for stage in tuple(live):
            if next(stage, live) is live:
                live.remove(stage)


def _params(sem):
    return pltpu.CompilerParams(dimension_semantics=sem, vmem_limit_bytes=VMEM_LIMIT)


def _norm_matmul_kernel(x_ref, g_ref, w_ref, o_ref, h_ref, *, lead_blocks, lead_scale):
    @pl.when(pl.program_id(1) == 0)
    def _():
        h_ref[...] = _rms(x_ref[...], g_ref[...]).astype(bf16)

    y = jnp.dot(h_ref[...], w_ref[...], preferred_element_type=f32)
    if lead_blocks:
        y = y * jnp.where(pl.program_id(1) < lead_blocks, lead_scale, 1.0)
    if len(o_ref.shape) == 2:
        o_ref[...] = y.astype(o_ref.dtype)
    else:
        for c in range(o_ref.shape[0]):
            o_ref[c] = y[:, c * LANES:(c + 1) * LANES].astype(o_ref.dtype)


def _norm_matmul(x, g, w, layer, out_dtype, lead_cols=0, lead_scale=1.0, column_major=False):
    t, d = x.shape
    n = w.shape[2]
    tm, tn = min(MM_TM, t), min(MM_TN, n)
    assert t % tm == 0 and n % tn == 0 and lead_cols % tn == 0 and tn % LANES == 0
    if column_major:
        out_spec = pl.BlockSpec((tn // LANES, tm, LANES), lambda i, j: (j, i, 0))
        out_shape = jax.ShapeDtypeStruct((n // LANES, t, LANES), out_dtype)
    else:
        out_spec = pl.BlockSpec((tm, tn), lambda i, j: (i, j))
        out_shape = jax.ShapeDtypeStruct((t, n), out_dtype)
    return pl.pallas_call(
        functools.partial(_norm_matmul_kernel, lead_blocks=lead_cols // tn, lead_scale=lead_scale),
        grid=(t // tm, n // tn),
        in_specs=[
            pl.BlockSpec((tm, d), lambda i, j: (i, 0)),
            pl.BlockSpec((1, d), lambda i, j: (0, 0)),
            pl.BlockSpec((None, d, tn), lambda i, j: (layer, 0, j)),
        ],
        out_specs=out_spec,
        out_shape=out_shape,
        scratch_shapes=[pltpu.VMEM((tm, d), bf16)],
        compiler_params=_params(("parallel", "arbitrary")),
        name="norm_matmul",
    )(x, g.reshape(1, d), w)


def _mix_out_kernel(a1_ref, a2_ref, w1_ref, w2_ref, g_ref, x_ref, o_ref):
    heads = lambda ref: jnp.concatenate([ref[h] for h in range(ref.shape[0])], axis=-1)
    y = jnp.dot(heads(a1_ref), w1_ref[...], preferred_element_type=f32)
    y = y + jnp.dot(heads(a2_ref), w2_ref[...], preferred_element_type=f32)
    o_ref[...] = x_ref[...] + _rms(y, g_ref[...])


def _mix_out(a1, a2, w, layer, g, x):
    t, d = x.shape
    k1, k2 = a1.shape[0] * a1.shape[2], a2.shape[0] * a2.shape[2]
    assert k1 == k2 and w.shape[1] == k1 + k2
    tm = min(OUT_TM, t)
    assert t % tm == 0
    return pl.pallas_call(
        _mix_out_kernel,
        grid=(t // tm,),
        in_specs=[
            pl.BlockSpec((a1.shape[0], tm, a1.shape[2]), lambda i: (0, i, 0)),
            pl.BlockSpec((a2.shape[0], tm, a2.shape[2]), lambda i: (0, i, 0)),
            pl.BlockSpec((None, k1, d), lambda i: (layer, 0, 0)),
            pl.BlockSpec((None, k2, d), lambda i: (layer, 1, 0)),
            pl.BlockSpec((1, d), lambda i: (0, 0)),
            pl.BlockSpec((tm, d), lambda i: (i, 0)),
        ],
        out_specs=pl.BlockSpec((tm, d), lambda i: (i, 0)),
        out_shape=jax.ShapeDtypeStruct((t, d), f32),
        compiler_params=_params(("parallel",)),
        name="mix_out",
    )(a1, a2, w, w, g.reshape(1, d), x)


def _t5_causal_bucket(dist):
    n = jnp.maximum(dist, 0)
    max_exact = REL_BUCKETS // 2
    nf = jnp.maximum(n, 1).astype(f32)
    large = max_exact + (jnp.log(nf / max_exact) / math.log(REL_MAX_DIST / max_exact)
                         * (REL_BUCKETS - max_exact)).astype(jnp.int32)
    large = jnp.minimum(large, REL_BUCKETS - 1)
    return jnp.where(n < max_exact, n, large)


def _toeplitz(vec, t):
    h = vec.shape[0]
    r = jnp.concatenate([vec[:, ::-1], jnp.zeros((h, 1), vec.dtype)], axis=1)
    g = jnp.tile(r, (1, t))[:, :t * (2 * t - 1)].reshape(h, t, 2 * t - 1)
    return g[:, :, t - 1:]


def _da_bias_tiles(rel_bias, tq):
    d = jnp.arange(-(tq - 1), tq)
    b0 = jnp.where((d >= 0)[:, None], rel_bias[_t5_causal_bucket(d)].astype(f32), MASK_VALUE)
    b1 = rel_bias[_t5_causal_bucket(d + tq)].astype(f32)
    return jnp.stack([_toeplitz(b0.T, tq), _toeplitz(b1.T, tq)], axis=1)


def _hg_step(f_ref, q_ref, i_ref, g_ref, lbl_ref, on_ref, o_ref, st_ref, kf_ref, bf_ref, vf_ref,
             *, layer, row0, tc, sb):
    groups = tc // sb
    piece = slice(row0, row0 + tc)

    lg = lbl_ref[...]
    pe = jnp.exp(lg - jnp.max(lg, axis=0, keepdims=True))
    lb = jnp.zeros((1, lg.shape[1]), f32)
    for r in range(1, layer + 1):
        lb = lb + pe[r:r + 1]
    lb = lb / jnp.sum(pe, axis=0, keepdims=True)

    f = lb + (1.0 - lb) * _sigmoid(f_ref[piece, :].astype(f32))
    k = 1.0 - f
    b = jnp.log2(f)
    r = lax.broadcasted_iota(jnp.int32, b.shape, 0) % sb
    sh = 1
    while sh < sb:
        b = b + jnp.where(r >= sh, pltpu.roll(b, sh, axis=0), 0.0)
        sh *= 2
    q = q_ref[piece, :].astype(f32)
    v = i_ref[piece, :].astype(f32)
    kf_ref[piece, :] = k
    bf_ref[piece, :] = b
    vf_ref[piece, :] = v

    def source_row(ref, s, g0):
        return jnp.stack([jnp.broadcast_to(ref[row0 + sb * g + s:row0 + sb * g + s + 1, :],
                                           (SUBLANES, ref.shape[1])) for g in range(g0, g0 + HG_BATCH)])

    def pieces(x):
        lo = jnp.stack([x[sb * g:sb * g + SUBLANES] for g in range(groups)])
        hi = jnp.stack([x[sb * g + SUBLANES:sb * (g + 1)] for g in range(groups)])
        return lo, hi

    q_lo, q_hi = pieces(q)
    k_lo, k_hi = pieces(k)
    b_lo, b_hi = pieces(b)
    b_end = b_hi[:, SUBLANES - 1:SUBLANES, :]

    qd_lo, qd_hi = q_lo * jnp.exp2(b_lo), q_hi * jnp.exp2(b_hi)
    kd_lo, kd_hi = k_lo * jnp.exp2(b_end - b_lo), k_hi * jnp.exp2(b_end - b_hi)
    block_decay = jnp.exp2(b_end[:, 0, :])
    vb = v.astype(bf16)
    increments = []
    for g in range(groups):
        kd = jnp.concatenate([kd_lo[g], kd_hi[g]], axis=0).astype(bf16)
        increments.append(lax.dot_general(vb[sb * g:sb * (g + 1)], kd, _TN, preferred_element_type=f32))
    st = st_ref[...]
    states = []
    for g in range(groups):
        states.append(st.astype(bf16))
        st = block_decay[g:g + 1] * st + increments[g]
    st_ref[...] = st
    o_state = []
    for g in range(groups):
        qd = jnp.concatenate([qd_lo[g], qd_hi[g]], axis=0).astype(bf16)
        o_state.append(lax.dot_general(qd, states[g], _NT, preferred_element_type=f32))

    trow = lax.broadcasted_iota(jnp.int32, (1, SUBLANES, 1), 1)

    def contrib(q_p, b_p, ks, bs, vs, diag_row):
        d = b_p - bs
        if diag_row is not None:
            d = jnp.where(trow >= diag_row, d, MASK_VALUE)
        a = jnp.sum(q_p * ks * jnp.exp2(d), axis=-1, keepdims=True)
        return a * vs

    o_blocks = []
    for g0 in range(0, groups, HG_BATCH):
        yield
        batch = slice(g0, g0 + HG_BATCH)
        ql, qh, bl, bh = q_lo[batch], q_hi[batch], b_lo[batch], b_hi[batch]
        o_lo = jnp.zeros(ql.shape, f32)
        o_hi = jnp.zeros(qh.shape, f32)
        for s in range(SUBLANES):
            ks, bs, vs = source_row(kf_ref, s, g0), source_row(bf_ref, s, g0), source_row(vf_ref, s, g0)
            o_lo = o_lo + contrib(ql, bl, ks, bs, vs, s)
            o_hi = o_hi + contrib(qh, bh, ks, bs, vs, None)
        for s in range(SUBLANES, sb):
            ks, bs, vs = source_row(kf_ref, s, g0), source_row(bf_ref, s, g0), source_row(vf_ref, s, g0)
            o_hi = o_hi + contrib(qh, bh, ks, bs, vs, s - SUBLANES)
        o_blocks += [jnp.concatenate([o_lo[g], o_hi[g]], axis=0) + o_state[g0 + g] for g in range(HG_BATCH)]
    o = jnp.concatenate(o_blocks, axis=0)
    yield

    gate = g_ref[piece, :].astype(f32)
    o_ref[piece, :] = (_rms(o, on_ref[...]) * (gate * _sigmoid(gate))).astype(o_ref.dtype)


def _mixer_kernel(q_ref, k_ref, v_ref, bias_ref, far_ref, lq1_ref, lk1_ref, lq2_ref, lk2_ref, sg_ref,
                  hf_ref, hq_ref, hi_ref, hg_ref, lbl_ref, on_ref,
                  oda_ref, ohg_ref,
                  m_ref, l_ref, acc_ref, p_ref, alpha_ref, st_ref, kf_ref, bf_ref, vf_ref,
                  *, tq, lam_init, layer, sb):
    i = pl.program_id(2)

    @pl.when(i == 0)
    def _():
        st_ref[...] = jnp.zeros(st_ref.shape, f32)

    q = q_ref[...]
    lane = lax.broadcasted_iota(jnp.int32, q.shape, 1)
    zero = jnp.zeros_like(q)
    q_maps = (jnp.where(lane < DA_QK_DIM, q, zero), jnp.where(lane >= DA_QK_DIM, q, zero))

    m_ref[...] = jnp.full(m_ref.shape, MASK_VALUE, f32)
    l_ref[...] = jnp.zeros(l_ref.shape, f32)
    acc_ref[...] = jnp.zeros(acc_ref.shape, f32)

    def rows(j):
        return pl.ds(pl.multiple_of(j * tq, tq), tq)

    whole_tile = ((0, tq, tq),)
    diagonal_tile = ((0, tq // 2, tq // 2), (tq // 2, tq // 2, tq))

    def unit(pv_block, pv_slot, qk_block, bias_tile, qk_slot, parts=whole_tile):
        if qk_block is not None:
            kb = k_ref[rows(qk_block), :]
            scores = [[lax.dot_general(q_maps[mi][r0:r0 + nr], kb[:nk], _NT, preferred_element_type=f32)
                       for mi in range(2)] for r0, nr, nk in parts]
        def value_matmul(mi):
            acc_ref[mi] = alpha_ref[pv_slot, mi] * acc_ref[mi] + jnp.dot(
                p_ref[pv_slot, mi], v_ref[rows(pv_block), :], preferred_element_type=f32)

        if pv_block is not None and qk_block is None:
            for mi in range(2):
                value_matmul(mi)
        if qk_block is not None:
            for n, ((r0, nr, nk), s_maps) in enumerate(zip(parts, scores)):
                rs = slice(r0, r0 + nr)
                for mi in range(2):
                    yield
                    if pv_block is not None and n == 0:
                        value_matmul(mi)
                    s = s_maps[mi] if bias_tile is None else s_maps[mi] + bias_ref[bias_tile, rs, :nk]
                    m_prev = m_ref[mi, rs]
                    m_new = jnp.maximum(m_prev, jnp.max(s, axis=-1, keepdims=True))
                    alpha = jnp.exp2(m_prev - m_new)
                    p = jnp.exp2(s - jnp.tile(m_new, (1, nk // LANES)))
                    l_ref[mi, rs] = alpha * l_ref[mi, rs] + jnp.sum(p, axis=-1, keepdims=True)
                    m_ref[mi, rs] = m_new
                    alpha_ref[qk_slot, mi, rs] = alpha
                    p_ref[qk_slot, mi, rs, :nk] = p.astype(p_ref.dtype)
                    if nk < tq:
                        p_ref[qk_slot, mi, rs, nk:] = jnp.zeros((nr, tq - nk), p_ref.dtype)

    hg_half = functools.partial(_hg_step, hf_ref, hq_ref, hi_ref, hg_ref, lbl_ref, on_ref, ohg_ref, st_ref,
                                kf_ref, bf_ref, vf_ref, layer=layer, tc=tq // 2, sb=sb)
    _alternate(unit(None, None, i, 0, 0, diagonal_tile), hg_half(row0=0))

    @pl.when(i == 0)
    def _():
        _alternate(unit(i, 0, None, None, None), hg_half(row0=tq // 2))

    @pl.when(i >= 1)
    def _():
        _alternate(unit(i, 0, i - 1, 1, 1), hg_half(row0=tq // 2))
        m_ref[...] = m_ref[...] - far_ref[...]
        n_far = i - 1

        def before(k):
            return jnp.where(k == 0, i - 1, k - 1)

        def far_pair(t, carry):
            _alternate(unit(before(2 * t), 1, 2 * t, None, 0))
            _alternate(unit(2 * t, 0, 2 * t + 1, None, 1))
            return carry

        lax.fori_loop(0, n_far // 2, far_pair, 0)

        @pl.when(n_far % 2 == 1)
        def _():
            _alternate(unit(before(n_far - 1), 1, n_far - 1, None, 0))
            _alternate(unit(n_far - 1, 0, None, None, None))

        @pl.when(n_far % 2 == 0)
        def _():
            _alternate(unit(before(n_far), 1, None, None, None))

    lam = (jnp.exp(jnp.sum(lq1_ref[...] * lk1_ref[...], axis=-1, keepdims=True))
           - jnp.exp(jnp.sum(lq2_ref[...] * lk2_ref[...], axis=-1, keepdims=True)) + lam_init)
    o = acc_ref[0] / l_ref[0] - lam * (acc_ref[1] / l_ref[1])
    oda_ref[...] = (_rms(o, sg_ref[...]) * (1.0 - lam_init)).astype(oda_ref.dtype)


def _token_mixer(proj, rel_bias, lq1, lk1, lq2, lk2, subln_g, lb_logits, onorm_g, *, batch, seq, layer, lam_init):
    tq = min(MIX_TILE, seq)
    assert seq % tq == 0 and tq >= REL_MAX_DIST and tq % (2 * LANES) == 0 and HG_SB == 2 * SUBLANES
    assert DA_HEADS == HG_HEADS and DA_V_DIM == HG_DK == HG_DV == LANES
    nq = seq // tq
    depth = lb_logits.shape[0]
    bias = _da_bias_tiles(rel_bias * LOG2E, tq)
    far = jnp.broadcast_to((rel_bias[REL_BUCKETS - 1].astype(f32) * LOG2E)[:, None, None], (DA_HEADS, 1, LANES))
    vec = lambda a: a.reshape(1, DA_QK_DIM).astype(f32)
    lam_spec = pl.BlockSpec((1, DA_QK_DIM), lambda b, h, i: (0, 0))
    tile = lambda group: pl.BlockSpec((None, tq, LANES), lambda b, h, i: (group * DA_HEADS + h, b * nq + i, 0))
    whole = lambda group: pl.BlockSpec((None, seq, LANES), lambda b, h, i: (group * DA_HEADS + h, b, 0))
    gain = pl.BlockSpec((1, LANES), lambda b, h, i: (0, 0))
    out = pl.BlockSpec((None, tq, LANES), lambda b, h, i: (h, b * nq + i, 0))
    return pl.pallas_call(
        functools.partial(_mixer_kernel, tq=tq, lam_init=lam_init, layer=layer, sb=HG_SB),
        grid=(batch, DA_HEADS, nq),
        in_specs=[
            tile(0), whole(1), whole(2),
            pl.BlockSpec((None, 2, tq, tq), lambda b, h, i: (h, 0, 0, 0)),
            pl.BlockSpec((None, 1, LANES), lambda b, h, i: (h, 0, 0)),
            lam_spec, lam_spec, lam_spec, lam_spec, gain,
            tile(3), tile(4), tile(5), tile(6),
            pl.BlockSpec((depth, LANES), lambda b, h, i: (0, h)),
            gain,
        ],
        out_specs=[out, out],
        out_shape=[jax.ShapeDtypeStruct((DA_HEADS, batch * seq, DA_V_DIM), bf16),
                   jax.ShapeDtypeStruct((HG_HEADS, batch * seq, HG_DV), bf16)],
        scratch_shapes=[
            pltpu.VMEM((2, tq, LANES), f32),
            pltpu.VMEM((2, tq, LANES), f32),
            pltpu.VMEM((2, tq, DA_V_DIM), f32),
            pltpu.VMEM((2, 2, tq, tq), bf16),
            pltpu.VMEM((2, 2, tq, LANES), f32),
            pltpu.VMEM((HG_DV, HG_DK), f32),
            pltpu.VMEM((tq, HG_DK), f32),
            pltpu.VMEM((tq, HG_DK), f32),
            pltpu.VMEM((tq, HG_DV), f32),
        ],
        compiler_params=_params(("parallel", "parallel", "arbitrary")),
        name="token_mixer",
    )(proj, proj, proj, bias, far, vec(lq1), vec(lk1), vec(lq2), vec(lk2), subln_g.reshape(1, DA_V_DIM),
      proj, proj, proj, proj, lb_logits.astype(f32), onorm_g.reshape(1, HG_DV))


def _cross_kernel(x_ref, gpre_ref, wq_ref, kv_ref, wo_ref, gpost_ref, o_ref):
    x = x_ref[...]
    h = _rms(x, gpre_ref[...]).astype(bf16)
    q = jnp.dot(h, wq_ref[...], preferred_element_type=f32).astype(bf16)
    outs = []
    for hd in range(CX_HEADS):
        cols = slice(hd * CX_HEAD_DIM, (hd + 1) * CX_HEAD_DIM)
        kh = kv_ref[:, cols]
        vh = kv_ref[:, CX_WIDTH + hd * CX_HEAD_DIM:CX_WIDTH + (hd + 1) * CX_HEAD_DIM]
        s = lax.dot_general(q[:, cols], kh, _NT, preferred_element_type=f32) * CX_HEAD_DIM ** -0.5
        p = jnp.exp(s - jnp.max(s, axis=-1, keepdims=True))
        oh = jnp.dot(p.astype(bf16), vh, preferred_element_type=f32)
        outs.append((oh / jnp.sum(p, axis=-1, keepdims=True)).astype(bf16))
    o = jnp.concatenate(outs, axis=-1)
    y = jnp.dot(o, wo_ref[...], preferred_element_type=f32)
    o_ref[...] = x + _rms(y, gpost_ref[...])


def _cross_attention(x, kv, wq, wo, layer, gpre, gpost, *, batch, seq, mem_len):
    t, d = x.shape
    tm = min(CX_TM, seq)
    assert seq % tm == 0
    per_batch = seq // tm
    return pl.pallas_call(
        _cross_kernel,
        grid=(t // tm,),
        in_specs=[
            pl.BlockSpec((tm, d), lambda i: (i, 0)),
            pl.BlockSpec((1, d), lambda i: (0, 0)),
            pl.BlockSpec((None, d, CX_WIDTH), lambda i: (layer, 0, 0)),
            pl.BlockSpec((mem_len, 2 * CX_WIDTH), lambda i: (i // per_batch, 0)),
            pl.BlockSpec((None, CX_WIDTH, d), lambda i: (layer, 0, 0)),
            pl.BlockSpec((1, d), lambda i: (0, 0)),
        ],
        out_specs=pl.BlockSpec((tm, d), lambda i: (i, 0)),
        out_shape=jax.ShapeDtypeStruct((t, d), f32),
        compiler_params=_params(("parallel",)),
        name="cross_attention",
    )(x, gpre.reshape(1, d), wq, kv, wo, gpost.reshape(1, d))


def _ffn_kernel(x_ref, gpre_ref, wg_ref, wu_ref, wo_ref, gpost_ref, o_ref, h_ref, acc_ref):
    j = pl.program_id(1)

    @pl.when(j == 0)
    def _():
        h_ref[...] = _rms(x_ref[...], gpre_ref[...]).astype(bf16)
        acc_ref[...] = jnp.zeros(acc_ref.shape, f32)

    h = h_ref[...]
    gate = jnp.dot(h, wg_ref[...], preferred_element_type=f32)
    up = jnp.dot(h, wu_ref[...], preferred_element_type=f32)
    a = (gate * _sigmoid(gate) * up).astype(bf16)
    acc_ref[...] += jnp.dot(a, wo_ref[...], preferred_element_type=f32)

    @pl.when(j == pl.num_programs(1) - 1)
    def _():
        o_ref[...] = x_ref[...] + _rms(acc_ref[...], gpost_ref[...])


def _ffn(x, w_in, w_out, layer, gpre, gpost):
    t, d = x.shape
    hidden = w_out.shape[1]
    tm, th = min(FFN_TM, t), min(FFN_TH, hidden)
    assert t % tm == 0 and hidden % th == 0 and w_in.shape[2] == 2 * hidden
    nh = hidden // th
    return pl.pallas_call(
        _ffn_kernel,
        grid=(t // tm, nh),
        in_specs=[
            pl.BlockSpec((tm, d), lambda i, j: (i, 0)),
            pl.BlockSpec((1, d), lambda i, j: (0, 0)),
            pl.BlockSpec((None, d, th), lambda i, j: (layer, 0, j)),
            pl.BlockSpec((None, d, th), lambda i, j: (layer, 0, nh + j)),
            pl.BlockSpec((None, th, d), lambda i, j: (layer, j, 0)),
            pl.BlockSpec((1, d), lambda i, j: (0, 0)),
        ],
        out_specs=pl.BlockSpec((tm, d), lambda i, j: (i, 0)),
        out_shape=jax.ShapeDtypeStruct((t, d), f32),
        scratch_shapes=[pltpu.VMEM((tm, d), bf16), pltpu.VMEM((tm, d), f32)],
        compiler_params=_params(("parallel", "arbitrary")),
        name="ffn",
    )(x, gpre.reshape(1, d), w_in, w_in, w_out, gpost.reshape(1, d))


def kernel(x, mem, w_in, w_out, w_cq, w_ckv, w_co, w_ffn_in, w_ffn_out, mix_pre_g, mix_post_g, cross_pre_g, cross_post_g, mem_norm_g, ffn_pre_g, ffn_post_g, da_subln_g, hg_onorm_g, lambda_q1, lambda_k1, lambda_q2, lambda_k2, hg_lb_logits, rel_bias):
    batch, seq, d = x.shape
    mem_len = mem.shape[1]
    depth = w_in.shape[0]
    xt = x.reshape(batch * seq, d)
    memt = mem.reshape(batch * mem_len, d)
    w_in, w_out, w_cq, w_ckv, w_co, w_ffn_in, w_ffn_out = (
        w.astype(bf16) for w in (w_in, w_out, w_cq, w_ckv, w_co, w_ffn_in, w_ffn_out))

    for l in range(depth):
        lam_init = 0.8 - 0.6 * math.exp(-0.3 * l)
        proj = _norm_matmul(xt, mix_pre_g[l], w_in, l, bf16,
                            lead_cols=DA_WIDTH, lead_scale=DA_QK_DIM ** -0.5 * LOG2E, column_major=True)
        o_da, o_hg = _token_mixer(proj, rel_bias, lambda_q1[l], lambda_k1[l], lambda_q2[l], lambda_k2[l],
                                  da_subln_g[l], hg_lb_logits, hg_onorm_g[l],
                                  batch=batch, seq=seq, layer=l, lam_init=lam_init)
        xt = _mix_out(o_da, o_hg, w_out, l, mix_post_g[l], xt)
        kv = _norm_matmul(memt, mem_norm_g[l], w_ckv, l, bf16)
        xt = _cross_attention(xt, kv, w_cq, w_co, l, cross_pre_g[l], cross_post_g[l],
                              batch=batch, seq=seq, mem_len=mem_len)
        xt = _ffn(xt, w_ffn_in, w_ffn_out, l, ffn_pre_g[l], ffn_post_g[l])

    return xt.reshape(batch, seq, d)
```

```python
import functools
import math

import jax
import jax.numpy as jnp
from jax import lax
from jax.experimental import pallas as pl
from jax.experimental.pallas import tpu as pltpu

NORM_EPS = 1e-6

DA_HEADS = 8
DA_QK_DIM = 64
DA_V_DIM = 2 * DA_QK_DIM
DA_WIDTH = DA_HEADS * DA_V_DIM
HG_HEADS = 8
HG_DK = 128
HG_DV = 128
HG_WIDTH = HG_HEADS * HG_DV
REL_BUCKETS = 32
REL_MAX_DIST = 128
CX_HEADS = 4
CX_HEAD_DIM = 128
CX_WIDTH = CX_HEADS * CX_HEAD_DIM

LANES = 128
SUBLANES = 8
MASK_VALUE = -1e30
LOG2E = math.log2(math.e)
VMEM_LIMIT = 56 * 1024 * 1024

MM_TM = 1024
MM_TN = 1024
OUT_TM = 512
CX_TM = 1024
FFN_TM = 512
FFN_TH = 512
MIX_TILE = 512
HG_SB = 16
HG_BATCH = 4

_NT = (((1,), (1,)), ((), ()))
_TN = (((0,), (0,)), ((), ()))

f32 = jnp.float32
bf16 = jnp.bfloat16


def _rms(x, g):
    return x * lax.rsqrt(jnp.mean(x * x, axis=-1, keepdims=True) + NORM_EPS) * g


def _sigmoid(x):
    return 1.0 / (1.0 + jnp.exp(-x))


def _alternate(*staged):
    live = list(staged)
    while live:
        for stage in tuple(live):
            if next(stage, live) is live:
                live.remove(stage)


def _params(sem):
    return pltpu.CompilerParams(dimension_semantics=sem, vmem_limit_bytes=VMEM_LIMIT)


def _norm_matmul_kernel(x_ref, g_ref, w_ref, o_ref, h_ref, *, lead_blocks, lead_scale):
    @pl.when(pl.program_id(1) == 0)
    def _():
        h_ref[...] = _rms(x_ref[...], g_ref[...]).astype(bf16)

    y = jnp.dot(h_ref[...], w_ref[...], preferred_element_type=f32)
    if lead_blocks:
        y = y * jnp.where(pl.program_id(1) < lead_blocks, lead_scale, 1.0)
    if len(o_ref.shape) == 2:
        o_ref[...] = y.astype(o_ref.dtype)
    else:
        for c in range(o_ref.shape[0]):
            o_ref[c] = y[:, c * LANES:(c + 1) * LANES].astype(o_ref.dtype)


def _norm_matmul(x, g, w, layer, out_dtype, lead_cols=0, lead_scale=1.0, column_major=False):
    t, d = x.shape
    n = w.shape[2]
    tm, tn = min(MM_TM, t), min(MM_TN, n)
    assert t % tm == 0 and n % tn == 0 and lead_cols % tn == 0 and tn % LANES == 0
    if column_major:
        out_spec = pl.BlockSpec((tn // LANES, tm, LANES), lambda i, j: (j, i, 0))
        out_shape = jax.ShapeDtypeStruct((n // LANES, t, LANES), out_dtype)
    else:
        out_spec = pl.BlockSpec((tm, tn), lambda i, j: (i, j))
        out_shape = jax.ShapeDtypeStruct((t, n), out_dtype)
    return pl.pallas_call(
        functools.partial(_norm_matmul_kernel, lead_blocks=lead_cols // tn, lead_scale=lead_scale),
        grid=(t // tm, n // tn),
        in_specs=[
            pl.BlockSpec((tm, d), lambda i, j: (i, 0)),
            pl.BlockSpec((1, d), lambda i, j: (0, 0)),
            pl.BlockSpec((None, d, tn), lambda i, j: (layer, 0, j)),
        ],
        out_specs=out_spec,
        out_shape=out_shape,
        scratch_shapes=[pltpu.VMEM((tm, d), bf16)],
        compiler_params=_params(("parallel", "arbitrary")),
        name="norm_matmul",
    )(x, g.reshape(1, d), w)


def _mix_out_kernel(a1_ref, a2_ref, w1_ref, w2_ref, g_ref, x_ref, o_ref):
    heads = lambda ref: jnp.concatenate([ref[h] for h in range(ref.shape[0])], axis=-1)
    y = jnp.dot(heads(a1_ref), w1_ref[...], preferred_element_type=f32)
    y = y + jnp.dot(heads(a2_ref), w2_ref[...], preferred_element_type=f32)
    o_ref[...] = x_ref[...] + _rms(y, g_ref[...])


def _mix_out(a1, a2, w, layer, g, x):
    t, d = x.shape
    k1, k2 = a1.shape[0] * a1.shape[2], a2.shape[0] * a2.shape[2]
    assert k1 == k2 and w.shape[1] == k1 + k2
    tm = min(OUT_TM, t)
    assert t % tm == 0
    return pl.pallas_call(
        _mix_out_kernel,
        grid=(t // tm,),
        in_specs=[
            pl.BlockSpec((a1.shape[0], tm, a1.shape[2]), lambda i: (0, i, 0)),
            pl.BlockSpec((a2.shape[0], tm, a2.shape[2]), lambda i: (0, i, 0)),
            pl.BlockSpec((None, k1, d), lambda i: (layer, 0, 0)),
            pl.BlockSpec((None, k2, d), lambda i: (layer, 1, 0)),
            pl.BlockSpec((1, d), lambda i: (0, 0)),
            pl.BlockSpec((tm, d), lambda i: (i, 0)),
        ],
        out_specs=pl.BlockSpec((tm, d), lambda i: (i, 0)),
        out_shape=jax.ShapeDtypeStruct((t, d), f32),
        compiler_params=_params(("parallel",)),
        name="mix_out",
    )(a1, a2, w, w, g.reshape(1, d), x)


def _t5_causal_bucket(dist):
    n = jnp.maximum(dist, 0)
    max_exact = REL_BUCKETS // 2
    nf = jnp.maximum(n, 1).astype(f32)
    large = max_exact + (jnp.log(nf / max_exact) / math.log(REL_MAX_DIST / max_exact)
                         * (REL_BUCKETS - max_exact)).astype(jnp.int32)
    large = jnp.minimum(large, REL_BUCKETS - 1)
    return jnp.where(n < max_exact, n, large)


def _toeplitz(vec, t):
    h = vec.shape[0]
    r = jnp.concatenate([vec[:, ::-1], jnp.zeros((h, 1), vec.dtype)], axis=1)
    g = jnp.tile(r, (1, t))[:, :t * (2 * t - 1)].reshape(h, t, 2 * t - 1)
    return g[:, :, t - 1:]


def _da_bias_tiles(rel_bias, tq):
    d = jnp.arange(-(tq - 1), tq)
    b0 = jnp.where((d >= 0)[:, None], rel_bias[_t5_causal_bucket(d)].astype(f32), MASK_VALUE)
    b1 = rel_bias[_t5_causal_bucket(d + tq)].astype(f32)
    return jnp.stack([_toeplitz(b0.T, tq), _toeplitz(b1.T, tq)], axis=1)


def _hg_step(f_ref, q_ref, i_ref, g_ref, lbl_ref, on_ref, o_ref, st_ref, kf_ref, bf_ref, vf_ref,
             *, layer, row0, tc, sb):
    groups = tc // sb
    piece = slice(row0, row0 + tc)

    lg = lbl_ref[...]
    pe = jnp.exp(lg - jnp.max(lg, axis=0, keepdims=True))
    lb = jnp.zeros((1, lg.shape[1]), f32)
    for r in range(1, layer + 1):
        lb = lb + pe[r:r + 1]
    lb = lb / jnp.sum(pe, axis=0, keepdims=True)

    f = lb + (1.0 - lb) * _sigmoid(f_ref[piece, :].astype(f32))
    k = 1.0 - f
    b = jnp.log2(f)
    r = lax.broadcasted_iota(jnp.int32, b.shape, 0) % sb
    sh = 1
    while sh < sb:
        b = b + jnp.where(r >= sh, pltpu.roll(b, sh, axis=0), 0.0)
        sh *= 2
    q = q_ref[piece, :].astype(f32)
    v = i_ref[piece, :].astype(f32)
    kf_ref[piece, :] = k
    bf_ref[piece, :] = b
    vf_ref[piece, :] = v

    def source_row(ref, s, g0):
        return jnp.stack([jnp.broadcast_to(ref[row0 + sb * g + s:row0 + sb * g + s + 1, :],
                                           (SUBLANES, ref.shape[1])) for g in range(g0, g0 + HG_BATCH)])

    def pieces(x):
        lo = jnp.stack([x[sb * g:sb * g + SUBLANES] for g in range(groups)])
        hi = jnp.stack([x[sb * g + SUBLANES:sb * (g + 1)] for g in range(groups)])
        return lo, hi

    q_lo, q_hi = pieces(q)
    k_lo, k_hi = pieces(k)
    b_lo, b_hi = pieces(b)
    b_end = b_hi[:, SUBLANES - 1:SUBLANES, :]

    qd_lo, qd_hi = q_lo * jnp.exp2(b_lo), q_hi * jnp.exp2(b_hi)
    kd_lo, kd_hi = k_lo * jnp.exp2(b_end - b_lo), k_hi * jnp.exp2(b_end - b_hi)
    block_decay = jnp.exp2(b_end[:, 0, :])
    vb = v.astype(bf16)
    increments = []
    for g in range(groups):
        kd = jnp.concatenate([kd_lo[g], kd_hi[g]], axis=0).astype(bf16)
        increments.append(lax.dot_general(vb[sb * g:sb * (g + 1)], kd, _TN, preferred_element_type=f32))
    st = st_ref[...]
    states = []
    for g in range(groups):
        states.append(st.astype(bf16))
        st = block_decay[g:g + 1] * st + increments[g]
    st_ref[...] = st
    o_state = []
    for g in range(groups):
        qd = jnp.concatenate([qd_lo[g], qd_hi[g]], axis=0).astype(bf16)
        o_state.append(lax.dot_general(qd, states[g], _NT, preferred_element_type=f32))

    trow = lax.broadcasted_iota(jnp.int32, (1, SUBLANES, 1), 1)

    def contrib(q_p, b_p, ks, bs, vs, diag_row):
        d = b_p - bs
        if diag_row is not None:
            d = jnp.where(trow >= diag_row, d, MASK_VALUE)
        a = jnp.sum(q_p * ks * jnp.exp2(d), axis=-1, keepdims=True)
        return a * vs

    o_blocks = []
    for g0 in range(0, groups, HG_BATCH):
        yield
        batch = slice(g0, g0 + HG_BATCH)
        ql, qh, bl, bh = q_lo[batch], q_hi[batch], b_lo[batch], b_hi[batch]
        o_lo = jnp.zeros(ql.shape, f32)
        o_hi = jnp.zeros(qh.shape, f32)
        for s in range(SUBLANES):
            ks, bs, vs = source_row(kf_ref, s, g0), source_row(bf_ref, s, g0), source_row(vf_ref, s, g0)
            o_lo = o_lo + contrib(ql, bl, ks, bs, vs, s)
            o_hi = o_hi + contrib(qh, bh, ks, bs, vs, None)
        for s in range(SUBLANES, sb):
            ks, bs, vs = source_row(kf_ref, s, g0), source_row(bf_ref, s, g0), source_row(vf_ref, s, g0)
            o_hi = o_hi + contrib(qh, bh, ks, bs, vs, s - SUBLANES)
        o_blocks += [jnp.concatenate([o_lo[g], o_hi[g]], axis=0) + o_state[g0 + g] for g in range(HG_BATCH)]
    o = jnp.concatenate(o_blocks, axis=0)
    yield

    gate = g_ref[piece, :].astype(f32)
    o_ref[piece, :] = (_rms(o, on_ref[...]) * (gate * _sigmoid(gate))).astype(o_ref.dtype)


def _mixer_kernel(q_ref, k_ref, v_ref, bias_ref, far_ref, lq1_ref, lk1_ref, lq2_ref, lk2_ref, sg_ref,
                  hf_ref, hq_ref, hi_ref, hg_ref, lbl_ref, on_ref,
                  oda_ref, ohg_ref,
                  m_ref, l_ref, acc_ref, p_ref, alpha_ref, st_ref, kf_ref, bf_ref, vf_ref,
                  *, tq, lam_init, layer, sb):
    i = pl.program_id(2)

    @pl.when(i == 0)
    def _():
        st_ref[...] = jnp.zeros(st_ref.shape, f32)

    q = q_ref[...]
    lane = lax.broadcasted_iota(jnp.int32, q.shape, 1)
    zero = jnp.zeros_like(q)
    q_maps = (jnp.where(lane < DA_QK_DIM, q, zero), jnp.where(lane >= DA_QK_DIM, q, zero))

    m_ref[...] = jnp.full(m_ref.shape, MASK_VALUE, f32)
    l_ref[...] = jnp.zeros(l_ref.shape, f32)
    acc_ref[...] = jnp.zeros(acc_ref.shape, f32)

    def rows(j):
        return pl.ds(pl.multiple_of(j * tq, tq), tq)

    whole_tile = ((0, tq, tq),)
    diagonal_tile = ((0, tq // 2, tq // 2), (tq // 2, tq // 2, tq))

    def unit(pv_block, pv_slot, qk_block, bias_tile, qk_slot, parts=whole_tile):
        if qk_block is not None:
            kb = k_ref[rows(qk_block), :]
            scores = [[lax.dot_general(q_maps[mi][r0:r0 + nr], kb[:nk], _NT, preferred_element_type=f32)
                       for mi in range(2)] for r0, nr, nk in parts]
        def value_matmul(mi):
            acc_ref[mi] = alpha_ref[pv_slot, mi] * acc_ref[mi] + jnp.dot(
                p_ref[pv_slot, mi], v_ref[rows(pv_block), :], preferred_element_type=f32)

        if pv_block is not None and qk_block is None:
            for mi in range(2):
                value_matmul(mi)
        if qk_block is not None:
            for n, ((r0, nr, nk), s_maps) in enumerate(zip(parts, scores)):
                rs = slice(r0, r0 + nr)
                for mi in range(2):
                    yield
                    if pv_block is not None and n == 0:
                        value_matmul(mi)
                    s = s_maps[mi] if bias_tile is None else s_maps[mi] + bias_ref[bias_tile, rs, :nk]
                    m_prev = m_ref[mi, rs]
                    m_new = jnp.maximum(m_prev, jnp.max(s, axis=-1, keepdims=True))
                    alpha = jnp.exp2(m_prev - m_new)
                    p = jnp.exp2(s - jnp.tile(m_new, (1, nk // LANES)))
                    l_ref[mi, rs] = alpha * l_ref[mi, rs] + jnp.sum(p, axis=-1, keepdims=True)
                    m_ref[mi, rs] = m_new
                    alpha_ref[qk_slot, mi, rs] = alpha
                    p_ref[qk_slot, mi, rs, :nk] = p.astype(p_ref.dtype)
                    if nk < tq:
                        p_ref[qk_slot, mi, rs, nk:] = jnp.zeros((nr, tq - nk), p_ref.dtype)

    hg_half = functools.partial(_hg_step, hf_ref, hq_ref, hi_ref, hg_ref, lbl_ref, on_ref, ohg_ref, st_ref,
                                kf_ref, bf_ref, vf_ref, layer=layer, tc=tq // 2, sb=sb)
    _alternate(unit(None, None, i, 0, 0, diagonal_tile), hg_half(row0=0))

    @pl.when(i == 0)
    def _():
        _alternate(unit(i, 0, None, None, None), hg_half(row0=tq // 2))

    @pl.when(i >= 1)
    def _():
        _alternate(unit(i, 0, i - 1, 1, 1), hg_half(row0=tq // 2))
        m_ref[...] = m_ref[...] - far_ref[...]
        n_far = i - 1

        def before(k):
            return jnp.where(k == 0, i - 1, k - 1)

        def far_pair(t, carry):
            _alternate(unit(before(2 * t), 1, 2 * t, None, 0))
            _alternate(unit(2 * t, 0, 2 * t + 1, None, 1))
            return carry

        lax.fori_loop(0, n_far // 2, far_pair, 0)

        @pl.when(n_far % 2 == 1)
        def _():
            _alternate(unit(before(n_far - 1), 1, n_far - 1, None, 0))
            _alternate(unit(n_far - 1, 0, None, None, None))

        @pl.when(n_far % 2 == 0)
        def _():
            _alternate(unit(before(n_far), 1, None, None, None))

    lam = (jnp.exp(jnp.sum(lq1_ref[...] * lk1_ref[...], axis=-1, keepdims=True))
           - jnp.exp(jnp.sum(lq2_ref[...] * lk2_ref[...], axis=-1, keepdims=True)) + lam_init)
    o = acc_ref[0] / l_ref[0] - lam * (acc_ref[1] / l_ref[1])
    oda_ref[...] = (_rms(o, sg_ref[...]) * (1.0 - lam_init)).astype(oda_ref.dtype)


def _token_mixer(proj, rel_bias, lq1, lk1, lq2, lk2, subln_g, lb_logits, onorm_g, *, batch, seq, layer, lam_init):
    tq = min(MIX_TILE, seq)
    assert seq % tq == 0 and tq >= REL_MAX_DIST and tq % (2 * LANES) == 0 and HG_SB == 2 * SUBLANES
    assert DA_HEADS == HG_HEADS and DA_V_DIM == HG_DK == HG_DV == LANES
    nq = seq // tq
    depth = lb_logits.shape[0]
    bias = _da_bias_tiles(rel_bias * LOG2E, tq)
    far = jnp.broadcast_to((rel_bias[REL_BUCKETS - 1].astype(f32) * LOG2E)[:, None, None], (DA_HEADS, 1, LANES))
    vec = lambda a: a.reshape(1, DA_QK_DIM).astype(f32)
    lam_spec = pl.BlockSpec((1, DA_QK_DIM), lambda b, h, i: (0, 0))
    tile = lambda group: pl.BlockSpec((None, tq, LANES), lambda b, h, i: (group * DA_HEADS + h, b * nq + i, 0))
    whole = lambda group: pl.BlockSpec((None, seq, LANES), lambda b, h, i: (group * DA_HEADS + h, b, 0))
    gain = pl.BlockSpec((1, LANES), lambda b, h, i: (0, 0))
    out = pl.BlockSpec((None, tq, LANES), lambda b, h, i: (h, b * nq + i, 0))
    return pl.pallas_call(
        functools.partial(_mixer_kernel, tq=tq, lam_init=lam_init, layer=layer, sb=HG_SB),
        grid=(batch, DA_HEADS, nq),
        in_specs=[
            tile(0), whole(1), whole(2),
            pl.BlockSpec((None, 2, tq, tq), lambda b, h, i: (h, 0, 0, 0)),
            pl.BlockSpec((None, 1, LANES), lambda b, h, i: (h, 0, 0)),
            lam_spec, lam_spec, lam_spec, lam_spec, gain,
            tile(3), tile(4), tile(5), tile(6),
            pl.BlockSpec((depth, LANES), lambda b, h, i: (0, h)),
            gain,
        ],
        out_specs=[out, out],
        out_shape=[jax.ShapeDtypeStruct((DA_HEADS, batch * seq, DA_V_DIM), bf16),
                   jax.ShapeDtypeStruct((HG_HEADS, batch * seq, HG_DV), bf16)],
        scratch_shapes=[
            pltpu.VMEM((2, tq, LANES), f32),
            pltpu.VMEM((2, tq, LANES), f32),
            pltpu.VMEM((2, tq, DA_V_DIM), f32),
            pltpu.VMEM((2, 2, tq, tq), bf16),
            pltpu.VMEM((2, 2, tq, LANES), f32),
            pltpu.VMEM((HG_DV, HG_DK), f32),
            pltpu.VMEM((tq, HG_DK), f32),
            pltpu.VMEM((tq, HG_DK), f32),
            pltpu.VMEM((tq, HG_DV), f32),
        ],
        compiler_params=_params(("parallel", "parallel", "arbitrary")),
        name="token_mixer",
    )(proj, proj, proj, bias, far, vec(lq1), vec(lk1), vec(lq2), vec(lk2), subln_g.reshape(1, DA_V_DIM),
      proj, proj, proj, proj, lb_logits.astype(f32), onorm_g.reshape(1, HG_DV))


def _cross_kernel(x_ref, gpre_ref, wq_ref, kv_ref, wo_ref, gpost_ref, o_ref):
    x = x_ref[...]
    h = _rms(x, gpre_ref[...]).astype(bf16)
    q = jnp.dot(h, wq_ref[...], preferred_element_type=f32).astype(bf16)
    outs = []
    for hd in range(CX_HEADS):
        cols = slice(hd * CX_HEAD_DIM, (hd + 1) * CX_HEAD_DIM)
        kh = kv_ref[:, cols]
        vh = kv_ref[:, CX_WIDTH + hd * CX_HEAD_DIM:CX_WIDTH + (hd + 1) * CX_HEAD_DIM]
        s = lax.dot_general(q[:, cols], kh, _NT, preferred_element_type=f32) * CX_HEAD_DIM ** -0.5
        p = jnp.exp(s - jnp.max(s, axis=-1, keepdims=True))
        oh = jnp.dot(p.astype(bf16), vh, preferred_element_type=f32)
        outs.append((oh / jnp.sum(p, axis=-1, keepdims=True)).astype(bf16))
    o = jnp.concatenate(outs, axis=-1)
    y = jnp.dot(o, wo_ref[...], preferred_element_type=f32)
    o_ref[...] = x + _rms(y, gpost_ref[...])


def _cross_attention(x, kv, wq, wo, layer, gpre, gpost, *, batch, seq, mem_len):
    t, d = x.shape
    tm = min(CX_TM, seq)
    assert seq % tm == 0
    per_batch = seq // tm
    return pl.pallas_call(
        _cross_kernel,
        grid=(t // tm,),
        in_specs=[
            pl.BlockSpec((tm, d), lambda i: (i, 0)),
            pl.BlockSpec((1, d), lambda i: (0, 0)),
            pl.BlockSpec((None, d, CX_WIDTH), lambda i: (layer, 0, 0), pipeline_mode=pl.Buffered(1)),
            pl.BlockSpec((mem_len, 2 * CX_WIDTH), lambda i: (i // per_batch, 0)),
            pl.BlockSpec((None, CX_WIDTH, d), lambda i: (layer, 0, 0), pipeline_mode=pl.Buffered(1)),
            pl.BlockSpec((1, d), lambda i: (0, 0)),
        ],
        out_specs=pl.BlockSpec((tm, d), lambda i: (i, 0)),
        out_shape=jax.ShapeDtypeStruct((t, d), f32),
        compiler_params=_params(("parallel",)),
        name="cross_attention",
    )(x, gpre.reshape(1, d), wq, kv, wo, gpost.reshape(1, d))


def _ffn_kernel(x_ref, gpre_ref, wg_ref, wu_ref, wo_ref, gpost_ref, o_ref, h_ref, acc_ref):
    j = pl.program_id(1)

    @pl.when(j == 0)
    def _():
        h_ref[...] = _rms(x_ref[...], gpre_ref[...]).astype(bf16)
        acc_ref[...] = jnp.zeros(acc_ref.shape, f32)

    h = h_ref[...]
    gate = jnp.dot(h, wg_ref[...], preferred_element_type=f32)
    up = jnp.dot(h, wu_ref[...], preferred_element_type=f32)
    a = (gate * _sigmoid(gate) * up).astype(bf16)
    acc_ref[...] += jnp.dot(a, wo_ref[...], preferred_element_type=f32)

    @pl.when(j == pl.num_programs(1) - 1)
    def _():
        o_ref[...] = x_ref[...] + _rms(acc_ref[...], gpost_ref[...])


def _ffn(x, w_in, w_out, layer, gpre, gpost):
    t, d = x.shape
    hidden = w_out.shape[1]
    tm, th = min(FFN_TM, t), min(FFN_TH, hidden)
    assert t % tm == 0 and hidden % th == 0 and w_in.shape[2] == 2 * hidden
    nh = hidden // th
    return pl.pallas_call(
        _ffn_kernel,
        grid=(t // tm, nh),
        in_specs=[
            pl.BlockSpec((tm, d), lambda i, j: (i, 0)),
            pl.BlockSpec((1, d), lambda i, j: (0, 0)),
            pl.BlockSpec((None, d, th), lambda i, j: (layer, 0, j)),
            pl.BlockSpec((None, d, th), lambda i, j: (layer, 0, nh + j)),
            pl.BlockSpec((None, th, d), lambda i, j: (layer, j, 0)),
            pl.BlockSpec((1, d), lambda i, j: (0, 0)),
        ],
        out_specs=pl.BlockSpec((tm, d), lambda i, j: (i, 0)),
        out_shape=jax.ShapeDtypeStruct((t, d), f32),
        scratch_shapes=[pltpu.VMEM((tm, d), bf16), pltpu.VMEM((tm, d), f32)],
        compiler_params=_params(("parallel", "arbitrary")),
        name="ffn",
    )(x, gpre.reshape(1, d), w_in, w_in, w_out, gpost.reshape(1, d))


def kernel(x, mem, w_in, w_out, w_cq, w_ckv, w_co, w_ffn_in, w_ffn_out, mix_pre_g, mix_post_g, cross_pre_g, cross_post_g, mem_norm_g, ffn_pre_g, ffn_post_g, da_subln_g, hg_onorm_g, lambda_q1, lambda_k1, lambda_q2, lambda_k2, hg_lb_logits, rel_bias):
    batch, seq, d = x.shape
    mem_len = mem.shape[1]
    depth = w_in.shape[0]
    xt = x.reshape(batch * seq, d)
    memt = mem.reshape(batch * mem_len, d)
    w_in, w_out, w_cq, w_ckv, w_co, w_ffn_in, w_ffn_out = (
        w.astype(bf16) for w in (w_in, w_out, w_cq, w_ckv, w_co, w_ffn_in, w_ffn_out))

    for l in range(depth):
        lam_init = 0.8 - 0.6 * math.exp(-0.3 * l)
        proj = _norm_matmul(xt, mix_pre_g[l], w_in, l, bf16,
                            lead_cols=DA_WIDTH, lead_scale=DA_QK_DIM ** -0.5 * LOG2E, column_major=True)
        o_da, o_hg = _token_mixer(proj, rel_bias, lambda_q1[l], lambda_k1[l], lambda_q2[l], lambda_k2[l],
                                  da_subln_g[l], hg_lb_logits, hg_onorm_g[l],
                                  batch=batch, seq=seq, layer=l, lam_init=lam_init)
        xt = _mix_out(o_da, o_hg, w_out, l, mix_post_g[l], xt)
        kv = _norm_matmul(memt, mem_norm_g[l], w_ckv, l, bf16)
        xt = _cross_attention(xt, kv, w_cq, w_co, l, cross_pre_g[l], cross_post_g[l],
                              batch=batch, seq=seq, mem_len=mem_len)
        xt = _ffn(xt, w_ffn_in, w_ffn_out, l, ffn_pre_g[l], ffn_post_g[l])

    return xt.reshape(batch, seq, d)
```
